```python
import math
import functools
import jax, jax.numpy as jnp
from jax import lax
import numpy as np

D_MODEL = 1024
BATCH = 8
SEQ = 4096
DEPTH = 2
DEC_BATCH = 32
DEC_SEQ = 1
PAST_LEN = 16384
PAGE_SIZE = 128

D_A = D_MODEL // 2
D_B = D_MODEL // 2
LRU_BLOCKS = 8
LRU_BLOCK = D_A // LRU_BLOCKS
LRU_C = 8.0
CONV_A_W = 4
CONV_B_W = 31
N_HEADS = 8
HEAD_DIM = 64
V_DIM = 2 * HEAD_DIM
ATT_WIDTH = N_HEADS * V_DIM
ROPE_THETA = 10000.0
Q_BLOCK = 128
ATT_SCALE = HEAD_DIM ** -0.5
NEG_INF = -1e30
D_FF = 2816
ALPHA = (2 * DEPTH) ** 0.25
BETA = (8 * DEPTH) ** -0.25
LN_EPS = 1e-5

kernel_name = "hawk_conformer_diffattn_decoder_step"


def layer_norm(x, g, b):
    xf = x.astype(jnp.float32)
    mu = jnp.mean(xf, -1, keepdims=True)
    var = jnp.mean(jnp.square(xf - mu), -1, keepdims=True)
    return ((xf - mu) * lax.rsqrt(var + LN_EPS) * g + b).astype(x.dtype)


def swiglu(x, wg, wu, wd):
    return (jax.nn.silu(x @ wg) * (x @ wu)) @ wd


def depthwise_causal_conv(x_hist, w, b):
    c = w.shape[1]
    out = lax.conv_general_dilated(
        x_hist, w.astype(x_hist.dtype)[:, None, :], window_strides=(1,), padding='VALID',
        dimension_numbers=('NWC', 'WIO', 'NWC'), feature_group_count=c)
    return out + b


def rope(x, pos):
    half = HEAD_DIM // 2
    inv = ROPE_THETA ** (-jnp.arange(half, dtype=jnp.float32) / half)
    ang = pos.astype(jnp.float32)[:, None] * inv[None, :]
    cos = jnp.cos(ang)[:, None, :]
    sin = jnp.sin(ang)[:, None, :]
    xf = x.astype(jnp.float32)
    x1, x2 = xf[..., :half], xf[..., half:]
    return jnp.concatenate([x1 * cos - x2 * sin, x2 * cos + x1 * sin], -1).astype(x.dtype)


def rg_lru(xc, h0, w_a, b_a, w_x, b_x, lam):
    bsz, t, _ = xc.shape
    xb = xc.reshape(bsz, t, LRU_BLOCKS, LRU_BLOCK)
    r = jax.nn.sigmoid(jnp.einsum('btni,nij->btnj', xb, w_a).reshape(bsz, t, D_A) + b_a)
    i = jax.nn.sigmoid(jnp.einsum('btni,nij->btnj', xb, w_x).reshape(bsz, t, D_A) + b_x)
    log_a = -LRU_C * r.astype(jnp.float32) * jax.nn.softplus(-lam.astype(jnp.float32))
    a = jnp.exp(log_a)
    u = jnp.sqrt(-jnp.expm1(2.0 * log_a)) * (i * xc).astype(jnp.float32)

    def step(h, au):
        a_t, u_t = au
        h = a_t * h + u_t
        return h, h

    h_t, hs = lax.scan(step, h0.astype(jnp.float32), (jnp.swapaxes(a, 0, 1), jnp.swapaxes(u, 0, 1)))
    return jnp.swapaxes(hs, 0, 1).astype(xc.dtype), h_t.astype(h0.dtype)


def rec_conv_mixer(x, h0, conv_a_buf, conv_b_buf, rec_p):
    (w_in, conv_a_w, conv_a_b, lru_w_a, lru_b_a, lru_w_x, lru_b_x, lru_lambda,
     conv_b_w, conv_b_b, conv_ln_g, conv_ln_b, w_out) = rec_p
    z = x @ w_in
    gate_a, xa, vb, gb = jnp.split(z, [D_A, 2 * D_A, 2 * D_A + D_B], axis=-1)
    xa_hist = jnp.concatenate([conv_a_buf.astype(xa.dtype), xa], axis=1)
    xa_c = depthwise_causal_conv(xa_hist, conv_a_w, conv_a_b)
    h_seq, h_last = rg_lru(xa_c, h0, lru_w_a, lru_b_a, lru_w_x, lru_b_x, lru_lambda)
    ya = h_seq * jax.nn.gelu(gate_a)
    u = vb * jax.nn.sigmoid(gb)
    u_hist = jnp.concatenate([conv_b_buf.astype(u.dtype), u], axis=1)
    ub = depthwise_causal_conv(u_hist, conv_b_w, conv_b_b)
    yb = jax.nn.silu(layer_norm(ub, conv_ln_g, conv_ln_b))
    y = jnp.concatenate([ya, yb], axis=-1) @ w_out
    return y, (h_last, xa_hist[:, -(CONV_A_W - 1):], u_hist[:, -(CONV_B_W - 1):])


def diff_attn_core(q, k, v, q_pos, k_pos, lam):
    s = jnp.einsum('bqhcd,bkhcd->bhcqk', q, k).astype(jnp.float32) * ATT_SCALE
    mask = k_pos[None, :] <= q_pos[:, None]
    p = jax.nn.softmax(jnp.where(mask, s, NEG_INF), axis=-1)
    a = p[:, :, 0] - lam * p[:, :, 1]
    return jnp.einsum('bhqk,bkhe->bqhe', a.astype(v.dtype), v)


def attend_prompt(q, k, v, pos, lam):
    bsz, t = q.shape[:2]
    nb = t // Q_BLOCK
    kh = k.reshape(bsz, t, N_HEADS, 2, HEAD_DIM)
    qb = jnp.moveaxis(q.reshape(bsz, nb, Q_BLOCK, N_HEADS, 2, HEAD_DIM), 1, 0)
    pb = pos.reshape(nb, Q_BLOCK)
    o = lax.map(lambda qp: diff_attn_core(qp[0], kh, v, qp[1], pos, lam), (qb, pb))
    return jnp.moveaxis(o, 0, 1).reshape(bsz, t, N_HEADS, V_DIM)


def attend_sample(q, k, v, pos, lam, cache_k, cache_v, page_table):
    bsz, t = q.shape[:2]
    past = page_table.shape[1] * PAGE_SIZE
    kp = cache_k[page_table].reshape(bsz, past, N_HEADS, 2, HEAD_DIM)
    vp = cache_v[page_table].reshape(bsz, past, N_HEADS, V_DIM)
    kn = k.reshape(bsz, t, N_HEADS, 2, HEAD_DIM)
    s_past = jnp.einsum('bqhcd,bkhcd->bhcqk', q, kp).astype(jnp.float32) * ATT_SCALE
    s_new = jnp.einsum('bqhcd,bkhcd->bhcqk', q, kn).astype(jnp.float32) * ATT_SCALE
    causal = pos[None, :] <= pos[:, None]
    s = jnp.concatenate([s_past, jnp.where(causal, s_new, NEG_INF)], axis=-1)
    p = jax.nn.softmax(s, axis=-1)
    a = (p[:, :, 0] - lam * p[:, :, 1]).astype(v.dtype)
    return (jnp.einsum('bhqk,bkhe->bqhe', a[..., :past], vp)
            + jnp.einsum('bhqk,bkhe->bqhe', a[..., past:], v))


def diff_attention_mixer(x, pos, attend, att_p, lambda_init):
    w_qkv, lq1, lk1, lq2, lk2, subln_g, w_o = att_p
    bsz, t, _ = x.shape
    q, k, v = jnp.split(x @ w_qkv, 3, axis=-1)
    q = rope(q.reshape(bsz, t, 2 * N_HEADS, HEAD_DIM), pos).reshape(bsz, t, N_HEADS, 2, HEAD_DIM)
    k = rope(k.reshape(bsz, t, 2 * N_HEADS, HEAD_DIM), pos).reshape(bsz, t, N_HEADS, 2 * HEAD_DIM)
    v = v.reshape(bsz, t, N_HEADS, V_DIM)
    f32 = jnp.float32
    lam = (jnp.exp(jnp.sum(lq1.astype(f32) * lk1.astype(f32)))
           - jnp.exp(jnp.sum(lq2.astype(f32) * lk2.astype(f32))) + lambda_init)
    o = attend(q, k, v, pos, lam).astype(f32)
    o = o * lax.rsqrt(jnp.mean(jnp.square(o), -1, keepdims=True) + LN_EPS) * subln_g * (1.0 - lambda_init)
    y = o.astype(x.dtype).reshape(bsz, t, ATT_WIDTH) @ w_o
    return y, k, v


def trunk(x, pos, h0, conv_a_buf, conv_b_buf, attend, nf_p, rec_p, att_p):
    ln_g, ln_b, w_gate, w_up, w_down = nf_p
    rec_state = None
    kv = None
    for layer in range(DEPTH):
        x = layer_norm(ALPHA * x + 0.5 * swiglu(x, w_gate[layer, 0], w_up[layer, 0], w_down[layer, 0]),
                       ln_g[layer, 0], ln_b[layer, 0])
        if layer % 2 == 0:
            m, rec_state = rec_conv_mixer(x, h0, conv_a_buf, conv_b_buf, rec_p)
        else:
            lambda_init = 0.8 - 0.6 * math.exp(-0.3 * layer)
            m, k_new, v_new = diff_attention_mixer(x, pos, attend, att_p, lambda_init)
            kv = (k_new, v_new)
        x = layer_norm(ALPHA * x + m, ln_g[layer, 1], ln_b[layer, 1])
        x = layer_norm(ALPHA * x + 0.5 * swiglu(x, w_gate[layer, 1], w_up[layer, 1], w_down[layer, 1]),
                       ln_g[layer, 2], ln_b[layer, 2])
    return x, rec_state, kv


def setup_inputs(seed: int = 0) -> dict:
    key = jax.random.key(seed)
    ks = jax.random.split(key, 40)
    f32 = jnp.float32
    n_pages = PAST_LEN // PAGE_SIZE
    n_pool = (DEC_BATCH * n_pages * 5) // 4

    def nrm(k, shape, scale):
        return jax.random.normal(k, shape, f32) * scale

    u = jax.random.uniform(ks[20], (D_A,), f32, 0.9, 0.999)
    s = u ** (1.0 / LRU_C)
    lru_lambda = jnp.log(s) - jnp.log1p(-s)
    page_table = jax.random.permutation(ks[7], n_pool)[:DEC_BATCH * n_pages].reshape(DEC_BATCH, n_pages).astype(jnp.int32)
    return {
        "x_prompt": nrm(ks[0], (BATCH, SEQ, D_MODEL), 1.0),
        "x_sample": nrm(ks[1], (DEC_BATCH, DEC_SEQ, D_MODEL), 1.0),
        "state_lru_h": nrm(ks[2], (DEC_BATCH, D_A), 0.5),
        "state_conv_a": nrm(ks[3], (DEC_BATCH, CONV_A_W - 1, D_A), 1.0),
        "state_conv_b": nrm(ks[4], (DEC_BATCH, CONV_B_W - 1, D_B), 0.5),
        "cache_k": nrm(ks[5], (n_pool, PAGE_SIZE, N_HEADS, 2 * HEAD_DIM), 1.0),
        "cache_v": nrm(ks[6], (n_pool, PAGE_SIZE, N_HEADS, V_DIM), 1.0),
        "page_table": page_table,
        "ln_g": 1.0 + nrm(ks[8], (DEPTH, 3, D_MODEL), 0.01),
        "ln_b": nrm(ks[9], (DEPTH, 3, D_MODEL), 0.01),
        "ffn_w_gate": nrm(ks[10], (DEPTH, 2, D_MODEL, D_FF), D_MODEL ** -0.5),
        "ffn_w_up": nrm(ks[11], (DEPTH, 2, D_MODEL, D_FF), D_MODEL ** -0.5),
        "ffn_w_down": nrm(ks[12], (DEPTH, 2, D_FF, D_MODEL), BETA * D_FF ** -0.5),
        "rec_w_in": nrm(ks[13], (D_MODEL, 2 * D_A + 2 * D_B), D_MODEL ** -0.5),
        "conv_a_w": nrm(ks[14], (CONV_A_W, D_A), CONV_A_W ** -0.5),
        "conv_a_b": nrm(ks[15], (D_A,), 0.01),
        "lru_w_a": nrm(ks[16], (LRU_BLOCKS, LRU_BLOCK, LRU_BLOCK), LRU_BLOCK ** -0.5),
        "lru_b_a": nrm(ks[17], (D_A,), 0.01),
        "lru_w_x": nrm(ks[18], (LRU_BLOCKS, LRU_BLOCK, LRU_BLOCK), LRU_BLOCK ** -0.5),
        "lru_b_x": nrm(ks[19], (D_A,), 0.01),
        "lru_lambda": lru_lambda,
        "conv_b_w": nrm(ks[21], (CONV_B_W, D_B), CONV_B_W ** -0.5),
        "conv_b_b": nrm(ks[22], (D_B,), 0.01),
        "conv_ln_g": 1.0 + nrm(ks[23], (D_B,), 0.01),
        "conv_ln_b": nrm(ks[24], (D_B,), 0.01),
        "rec_w_out": nrm(ks[25], (D_A + D_B, D_MODEL), BETA * (D_A + D_B) ** -0.5),
        "att_w_qkv": nrm(ks[26], (D_MODEL, 3 * ATT_WIDTH), D_MODEL ** -0.5),
        "lambda_q1": nrm(ks[27], (HEAD_DIM,), 0.1),
        "lambda_k1": nrm(ks[28], (HEAD_DIM,), 0.1),
        "lambda_q2": nrm(ks[29], (HEAD_DIM,), 0.1),
        "lambda_k2": nrm(ks[30], (HEAD_DIM,), 0.1),
        "subln_g": 1.0 + nrm(ks[31], (V_DIM,), 0.01),
        "att_w_out": nrm(ks[32], (ATT_WIDTH, D_MODEL), BETA * ATT_WIDTH ** -0.5),
    }


def reference(x_prompt, x_sample, state_lru_h, state_conv_a, state_conv_b, cache_k, cache_v, page_table,
              ln_g, ln_b, ffn_w_gate, ffn_w_up, ffn_w_down,
              rec_w_in, conv_a_w, conv_a_b, lru_w_a, lru_b_a, lru_w_x, lru_b_x, lru_lambda,
              conv_b_w, conv_b_b, conv_ln_g, conv_ln_b, rec_w_out,
              att_w_qkv, lambda_q1, lambda_k1, lambda_q2, lambda_k2, subln_g, att_w_out):
    nf_p = (ln_g, ln_b, ffn_w_gate, ffn_w_up, ffn_w_down)
    rec_p = (rec_w_in, conv_a_w, conv_a_b, lru_w_a, lru_b_a, lru_w_x, lru_b_x, lru_lambda,
             conv_b_w, conv_b_b, conv_ln_g, conv_ln_b, rec_w_out)
    att_p = (att_w_qkv, lambda_q1, lambda_k1, lambda_q2, lambda_k2, subln_g, att_w_out)

    bp, tp = x_prompt.shape[:2]
    pos_p = jnp.arange(tp, dtype=jnp.int32)
    y_prompt, (h_p, ca_p, cb_p), (k_p, v_p) = trunk(
        x_prompt, pos_p,
        jnp.zeros((bp, D_A), state_lru_h.dtype),
        jnp.zeros((bp, CONV_A_W - 1, D_A), x_prompt.dtype),
        jnp.zeros((bp, CONV_B_W - 1, D_B), x_prompt.dtype),
        attend_prompt, nf_p, rec_p, att_p)

    past = page_table.shape[1] * PAGE_SIZE
    pos_s = past + jnp.arange(x_sample.shape[1], dtype=jnp.int32)
    attend_s = functools.partial(attend_sample, cache_k=cache_k, cache_v=cache_v, page_table=page_table)
    y_sample, (h_s, ca_s, cb_s), (k_s, v_s) = trunk(
        x_sample, pos_s, state_lru_h, state_conv_a, state_conv_b,
        attend_s, nf_p, rec_p, att_p)

    return (y_prompt, y_sample, h_p, ca_p, cb_p, k_p, v_p, h_s, ca_s, cb_s, k_s, v_s)
```

```python
import functools
import math

import jax
import jax.numpy as jnp
from jax import lax
from jax.experimental import pallas as pl
from jax.experimental.pallas import tpu as pltpu

F32 = jnp.float32
BF16 = jnp.bfloat16

DEPTH = 2
LRU_BLOCKS = 8
LRU_C = 8.0
HEAD_DIM = 64
ROPE_THETA = 10000.0
ATT_SCALE = HEAD_DIM ** -0.5
NEG_INF = -1e30
ALPHA = (2 * DEPTH) ** 0.25
LN_EPS = 1e-5
LAMBDA_INIT = 0.8 - 0.6 * math.exp(-0.3 * 1)

LANES = 128
SUBLANES = 8
VMEM_LIMIT_BYTES = 56 * 1024 * 1024

FFN_CHUNK = 256
FFN_ROWS = 512
MIX_ROWS = 256
QKV_ROWS = 512
ATT_Q = 256
ATT_K = 256
PROJ_ROWS = 512
DEC_PAGES = 4


def _layer_norm(y, g, b):
    mu = jnp.mean(y, axis=-1, keepdims=True)
    d = y - mu
    var = jnp.mean(d * d, axis=-1, keepdims=True)
    return d * lax.rsqrt(var + LN_EPS) * g + b


def _sigmoid(x):
    return 1.0 / (1.0 + jnp.exp(-x))


def _dot(a, b):
    return jnp.dot(a, b, preferred_element_type=F32)


def _dot_nt(a, b):
    return lax.dot_general(a, b, (((1,), (1,)), ((), ())), preferred_element_type=F32)


def _const_spec(shape):
    nd = len(shape)
    return pl.BlockSpec(shape, lambda *_: (0,) * nd)


def _params(*sem):
    return pltpu.CompilerParams(dimension_semantics=sem, vmem_limit_bytes=VMEM_LIMIT_BYTES)


def _ffn_kernel(x_ref, wg_ref, wu_ref, wd_ref, g_ref, b_ref, o_ref, *, n_chunks):
    x = x_ref[...]
    xb = x.astype(BF16)
    acc = None
    for c in range(n_chunks):
        g = _dot(xb, wg_ref[c])
        u = _dot(xb, wu_ref[c])
        a = (g * _sigmoid(g) * u).astype(BF16)
        d = _dot(a, wd_ref[c])
        acc = d if acc is None else acc + d
    o_ref[...] = _layer_norm(ALPHA * x + 0.5 * acc, g_ref[...], b_ref[...])


def _ffn_sublayer(x, wg, wu, wd, g, b):
    m, d = x.shape
    n_chunks, _, fc = wg.shape
    tm = min(FFN_ROWS, m)
    return pl.pallas_call(
        functools.partial(_ffn_kernel, n_chunks=n_chunks),
        grid=(pl.cdiv(m, tm),),
        in_specs=[
            pl.BlockSpec((tm, d), lambda i: (i, 0)),
            _const_spec((n_chunks, d, fc)),
            _const_spec((n_chunks, d, fc)),
            _const_spec((n_chunks, fc, d)),
            _const_spec((1, d)),
            _const_spec((1, d)),
        ],
        out_specs=pl.BlockSpec((tm, d), lambda i: (i, 0)),
        out_shape=jax.ShapeDtypeStruct((m, d), F32),
        compiler_params=_params("parallel"),
        name="ffn_sublayer",
    )(x, wg, wu, wd, g, b)


def _softplus(x):
    return jnp.maximum(x, 0.0) + jnp.log1p(jnp.exp(-jnp.abs(x)))


def _gelu_tanh(x):
    c = math.sqrt(2.0 / math.pi)
    return 0.5 * x * (1.0 + jnp.tanh(c * (x + 0.044715 * (x * x * x))))


def _lru_gates(xa_c, wa, ba, wx, bx, lam):
    xb = xa_c.astype(BF16)
    r = _sigmoid(_dot(xb, wa) + ba)
    i = _sigmoid(_dot(xb, wx) + bx)
    log_a = (-LRU_C) * r * _softplus(-lam)
    a = jnp.exp(log_a)
    th = jnp.tanh(log_a)
    u = jnp.sqrt(-2.0 * th / (1.0 - th)) * (i * xa_c)
    return a, u


def _conv_b_post(acc, g, b):
    y = _layer_norm(acc, g, b)
    return y * _sigmoid(y)


HIST_A = 8
HIST_B = 32


def _mixer_prompt_kernel(x_ref, h0_ref, ca0_ref, cb0_ref, w_in_ref, caw_ref, cab_ref,
                         wa_ref, ba_ref, wx_ref, bx_ref, lam_ref, cbw_ref, cbb_ref,
                         clg_ref, clb_ref, wout_ref, g_ref, b_ref,
                         y_ref, hl_ref, cas_ref, cbs_ref,
                         ha_ref, hb_ref, sa_ref, su_ref, hc_ref, *, tm, pad, ka, kb):
    d_a = ha_ref.shape[1]
    t = pl.program_id(1)

    @pl.when(t == 0)
    def _():
        ha_ref[HIST_A - (ka - 1):HIST_A, :] = ca0_ref[...]
        hb_ref[HIST_B - (kb - 1):HIST_B, :] = cb0_ref[...]
        hc_ref[...] = h0_ref[...]
        sa_ref[0:pad, :] = jnp.ones((pad, d_a), F32)
        su_ref[0:pad, :] = jnp.zeros((pad, d_a), F32)

    x = x_ref[...]
    z = _dot(x.astype(BF16), w_in_ref[...])
    gate_a = z[:, 0:d_a]
    xa = z[:, d_a:2 * d_a]
    vb = z[:, 2 * d_a:3 * d_a]
    gb = z[:, 3 * d_a:4 * d_a]

    ha_ref[HIST_A:HIST_A + tm, :] = xa
    xa_c = cab_ref[...]
    for j in range(ka):
        s = HIST_A - (ka - 1) + j
        xa_c = xa_c + caw_ref[j:j + 1, :] * ha_ref[s:s + tm, :]
    a, u = _lru_gates(xa_c, wa_ref[...], ba_ref[...], wx_ref[...], bx_ref[...], lam_ref[...])
    sa_ref[pad:pad + tm, :] = a
    su_ref[pad:pad + tm, :] = u
    su_ref[pad:pad + 1, :] = u[0:1, :] + a[0:1, :] * hc_ref[...]
    k = 1
    while k < tm:
        a_cur = sa_ref[pad:pad + tm, :]
        u_cur = su_ref[pad:pad + tm, :]
        a_sh = sa_ref[pad - k:pad - k + tm, :]
        u_sh = su_ref[pad - k:pad - k + tm, :]
        su_ref[pad:pad + tm, :] = u_cur + a_cur * u_sh
        if 2 * k < tm:
            sa_ref[pad:pad + tm, :] = a_cur * a_sh
        k *= 2
    h_seq = su_ref[pad:pad + tm, :]
    h_last = h_seq[tm - 1:tm, :]
    hc_ref[...] = h_last
    hl_ref[...] = h_last
    ya = h_seq * _gelu_tanh(gate_a)

    hb_ref[HIST_B:HIST_B + tm, :] = vb * _sigmoid(gb)
    acc = cbb_ref[...]
    for j in range(kb):
        s = HIST_B - (kb - 1) + j
        acc = acc + cbw_ref[j:j + 1, :] * hb_ref[s:s + tm, :]
    yb = _conv_b_post(acc, clg_ref[...], clb_ref[...])

    m = _dot(ya.astype(BF16), wout_ref[0:d_a, :]) + _dot(yb.astype(BF16), wout_ref[d_a:2 * d_a, :])
    y_ref[...] = _layer_norm(ALPHA * x + m, g_ref[...], b_ref[...])

    tail_a = ha_ref[tm + HIST_A - (ka - 1):tm + HIST_A, :]
    cas_ref[...] = tail_a
    ha_ref[HIST_A - (ka - 1):HIST_A, :] = tail_a
    tail_b = hb_ref[tm + HIST_B - (kb - 1):tm + HIST_B, :]
    cbs_ref[...] = tail_b
    hb_ref[HIST_B - (kb - 1):HIST_B, :] = tail_b


def _mixer_prompt(x, h0, ca0, cb0, mp, g, b):
    bsz, t, d = x.shape
    d_a = h0.shape[-1]
    ka = ca0.shape[1] + 1
    kb = cb0.shape[1] + 1
    assert ka - 1 <= HIST_A and kb - 1 <= HIST_B
    tm = min(MIX_ROWS, t)
    assert t % tm == 0
    pad = max(SUBLANES, tm // 2)
    kernel = functools.partial(_mixer_prompt_kernel, tm=tm, pad=pad, ka=ka, kb=kb)
    bt = lambda bi, ti: (bi, ti, 0)
    b0 = lambda bi, ti: (bi, 0, 0)
    outs = pl.pallas_call(
        kernel,
        grid=(bsz, t // tm),
        in_specs=[
            pl.BlockSpec((None, tm, d), bt),
            pl.BlockSpec((None, 1, d_a), b0),
            pl.BlockSpec((None, ka - 1, d_a), b0),
            pl.BlockSpec((None, kb - 1, d_a), b0),
            _const_spec(mp["w_in"].shape),
            _const_spec((ka, d_a)), _const_spec((1, d_a)),
            _const_spec((d_a, d_a)), _const_spec((1, d_a)),
            _const_spec((d_a, d_a)), _const_spec((1, d_a)),
            _const_spec((1, d_a)),
            _const_spec((kb, d_a)), _const_spec((1, d_a)),
            _const_spec((1, d_a)), _const_spec((1, d_a)),
            _const_spec(mp["w_out"].shape),
            _const_spec((1, d)), _const_spec((1, d)),
        ],
        out_specs=[
            pl.BlockSpec((None, tm, d), bt),
            pl.BlockSpec((None, 1, d_a), b0),
            pl.BlockSpec((None, ka - 1, d_a), b0),
            pl.BlockSpec((None, kb - 1, d_a), b0),
        ],
        out_shape=[
            jax.ShapeDtypeStruct((bsz, t, d), F32),
            jax.ShapeDtypeStruct((bsz, 1, d_a), F32),
            jax.ShapeDtypeStruct((bsz, ka - 1, d_a), F32),
            jax.ShapeDtypeStruct((bsz, kb - 1, d_a), F32),
        ],
        scratch_shapes=[
            pltpu.VMEM((HIST_A + tm, d_a), F32),
            pltpu.VMEM((HIST_B + tm, d_a), F32),
            pltpu.VMEM((pad + tm, d_a), F32),
            pltpu.VMEM((pad + tm, d_a), F32),
            pltpu.VMEM((1, d_a), F32),
        ],
        compiler_params=_params("parallel", "arbitrary"),
        name="mixer_prompt",
    )(x, h0.reshape(bsz, 1, d_a), ca0, cb0, mp["w_in"], mp["conv_a_w"], mp["conv_a_b"],
      mp["wa"], mp["ba"], mp["wx"], mp["bx"], mp["lam"], mp["conv_b_w"], mp["conv_b_b"],
      mp["cln_g"], mp["cln_b"], mp["w_out"], g, b)
    y, hl, cas, cbs = outs
    return y, hl.reshape(bsz, d_a), cas, cbs


def _mixer_sample_kernel(x_ref, h0_ref, ca_ref, cb_ref, w_in_ref, caw_ref, cab_ref,
                         wa_ref, ba_ref, wx_ref, bx_ref, lam_ref, cbw_ref, cbb_ref,
                         clg_ref, clb_ref, wout_ref, g_ref, b_ref,
                         y_ref, hl_ref, cas_ref, cbs_ref, *, ka, kb):
    d_a = h0_ref.shape[1]
    x = x_ref[...]
    z = _dot(x.astype(BF16), w_in_ref[...])
    gate_a = z[:, 0:d_a]
    xa = z[:, d_a:2 * d_a]
    vb = z[:, 2 * d_a:3 * d_a]
    gb = z[:, 3 * d_a:4 * d_a]

    xa_c = cab_ref[...]
    for j in range(ka - 1):
        xa_c = xa_c + caw_ref[j:j + 1, :] * ca_ref[j]
    xa_c = xa_c + caw_ref[ka - 1:ka, :] * xa
    a, u = _lru_gates(xa_c, wa_ref[...], ba_ref[...], wx_ref[...], bx_ref[...], lam_ref[...])
    h = a * h0_ref[...] + u
    hl_ref[...] = h
    ya = h * _gelu_tanh(gate_a)

    ub = vb * _sigmoid(gb)
    acc = cbb_ref[...]
    for j in range(kb - 1):
        acc = acc + cbw_ref[j:j + 1, :] * cb_ref[j]
    acc = acc + cbw_ref[kb - 1:kb, :] * ub
    yb = _conv_b_post(acc, clg_ref[...], clb_ref[...])

    m = _dot(ya.astype(BF16), wout_ref[0:d_a, :]) + _dot(yb.astype(BF16), wout_ref[d_a:2 * d_a, :])
    y_ref[...] = _layer_norm(ALPHA * x + m, g_ref[...], b_ref[...])

    for j in range(ka - 2):
        cas_ref[j] = ca_ref[j + 1]
    cas_ref[ka - 2] = xa
    for j in range(kb - 2):
        cbs_ref[j] = cb_ref[j + 1]
    cbs_ref[kb - 2] = ub


def _mixer_sample(x, h0, ca0, cb0, mp, g, b):
    n, d = x.shape
    d_a = h0.shape[-1]
    ka = ca0.shape[1] + 1
    kb = cb0.shape[1] + 1
    ca_t = jnp.swapaxes(ca0, 0, 1)
    cb_t = jnp.swapaxes(cb0, 0, 1)
    args = (x, h0, ca_t, cb_t, mp["w_in"], mp["conv_a_w"], mp["conv_a_b"],
            mp["wa"], mp["ba"], mp["wx"], mp["bx"], mp["lam"], mp["conv_b_w"], mp["conv_b_b"],
            mp["cln_g"], mp["cln_b"], mp["w_out"], g, b)
    y, hl, cas, cbs = pl.pallas_call(
        functools.partial(_mixer_sample_kernel, ka=ka, kb=kb),
        grid=(1,),
        in_specs=[_const_spec(a.shape) for a in args],
        out_specs=[_const_spec((n, d)), _const_spec((n, d_a)),
                   _const_spec((ka - 1, n, d_a)), _const_spec((kb - 1, n, d_a))],
        out_shape=[
            jax.ShapeDtypeStruct((n, d), F32),
            jax.ShapeDtypeStruct((n, d_a), F32),
            jax.ShapeDtypeStruct((ka - 1, n, d_a), F32),
            jax.ShapeDtypeStruct((kb - 1, n, d_a), F32),
        ],
        compiler_params=_params("arbitrary"),
        name="mixer_sample",
    )(*args)
    return y, hl, jnp.swapaxes(cas, 0, 1), jnp.swapaxes(cbs, 0, 1)


def _rope_tables(pos):
    half = HEAD_DIM // 2
    inv = ROPE_THETA ** (-jnp.arange(half, dtype=F32) / half)
    ang = pos.astype(F32)[:, None] * inv[None, :]
    cos = jnp.cos(ang)
    sin = jnp.sin(ang)
    c = jnp.concatenate([cos, cos, cos, cos], axis=-1)
    s = jnp.concatenate([-sin, sin, -sin, sin], axis=-1)
    return c, s


def _qkv_rope_kernel(x_ref, w_ref, cos_ref, sin_ref, k_ref, v_ref, qb_ref, kb_ref, vb_ref,
                     *, n_heads):
    aw = n_heads * 2 * HEAD_DIM
    z = _dot(x_ref[...].astype(BF16), w_ref[...])
    c = cos_ref[...]
    s = sin_ref[...]
    hw = 2 * HEAD_DIM
    lane = lax.broadcasted_iota(jnp.int32, (1, hw), 1)
    first_half = (lane % HEAD_DIM) < (HEAD_DIM // 2)

    def rot(xh):
        swapped = jnp.where(first_half, pltpu.roll(xh, hw - HEAD_DIM // 2, axis=1),
                            pltpu.roll(xh, HEAD_DIM // 2, axis=1))
        return xh * c + swapped * s

    for h in range(n_heads):
        sl = slice(h * hw, (h + 1) * hw)
        qh = rot(z[:, h * hw:(h + 1) * hw])
        kh = rot(z[:, aw + h * hw:aw + (h + 1) * hw])
        qb_ref[:, sl] = (qh * ATT_SCALE).astype(BF16)
        k_ref[:, sl] = kh
        kb_ref[:, sl] = kh.astype(BF16)
    v = z[:, 2 * aw:3 * aw]
    v_ref[...] = v
    vb_ref[...] = v.astype(BF16)


def _qkv_rope(x, w_qkv, cos, sin, n_heads, t_per_seq):
    m, d = x.shape
    aw = n_heads * 2 * HEAD_DIM
    hw = 2 * HEAD_DIM
    if t_per_seq == 1:
        tm = m
        tab_spec = pl.BlockSpec((1, hw), lambda i: (0, 0))
    else:
        tm = min(QKV_ROWS, t_per_seq)
        assert t_per_seq % tm == 0
        nt = t_per_seq // tm
        tab_spec = pl.BlockSpec((tm, hw), lambda i: (i % nt, 0))
    row = pl.BlockSpec((tm, aw), lambda i: (i, 0))
    return pl.pallas_call(
        functools.partial(_qkv_rope_kernel, n_heads=n_heads),
        grid=(m // tm,),
        in_specs=[pl.BlockSpec((tm, d), lambda i: (i, 0)), _const_spec(w_qkv.shape),
                  tab_spec, tab_spec],
        out_specs=[row, row, row, row, row],
        out_shape=[jax.ShapeDtypeStruct((m, aw), F32), jax.ShapeDtypeStruct((m, aw), F32),
                   jax.ShapeDtypeStruct((m, aw), BF16), jax.ShapeDtypeStruct((m, aw), BF16),
                   jax.ShapeDtypeStruct((m, aw), BF16)],
        compiler_params=_params("parallel"),
        name="qkv_rope",
    )(x, w_qkv, cos, sin)


def _diff_lambda(lq1, lk1, lq2, lk2):
    s1 = jnp.sum(lq1 * lk1, axis=-1, keepdims=True)
    s2 = jnp.sum(lq2 * lk2, axis=-1, keepdims=True)
    return jnp.exp(s1) - jnp.exp(s2) + LAMBDA_INIT


def _sub_norm(o, g):
    return o * lax.rsqrt(jnp.mean(o * o, axis=-1, keepdims=True) + LN_EPS) * g * (1.0 - LAMBDA_INIT)


def _attn_prompt_kernel(q_ref, k_ref, v_ref, lq1_ref, lk1_ref, lq2_ref, lk2_ref, sg_ref, o_ref,
                        *, tq, tk):
    i = pl.program_id(2)
    q = q_ref[...]
    lane = lax.broadcasted_iota(jnp.int32, q.shape, 1)
    zero = jnp.zeros_like(q)
    q2 = jnp.concatenate([jnp.where(lane < HEAD_DIM, q, zero),
                          jnp.where(lane >= HEAD_DIM, q, zero)], axis=0)
    rows = 2 * tq
    row_pos = lax.broadcasted_iota(jnp.int32, (rows, tk), 0) % tq
    col_pos = lax.broadcasted_iota(jnp.int32, (rows, tk), 1)
    ratio = tq // tk

    def step(j, carry, masked):
        m, l, acc = carry
        off = pl.multiple_of(j * tk, tk)
        kt = k_ref[pl.ds(off, tk), :]
        vt = v_ref[pl.ds(off, tk), :]
        s = _dot_nt(q2, kt)
        if masked:
            s = jnp.where(col_pos + (j - i * ratio) * tk <= row_pos, s, NEG_INF)
        m_new = jnp.maximum(m, jnp.max(s, axis=-1, keepdims=True))
        alpha = jnp.exp(m - m_new)
        p = jnp.exp(s - m_new)
        l = alpha * l + jnp.sum(p, axis=-1, keepdims=True)
        acc = alpha * acc + _dot(p.astype(BF16), vt)
        return m_new, l, acc

    init = (jnp.full((rows, 1), NEG_INF, F32), jnp.zeros((rows, 1), F32),
            jnp.zeros((rows, v_ref.shape[1]), F32))
    carry = lax.fori_loop(0, i * ratio, functools.partial(step, masked=False), init)
    for jj in range(ratio):
        carry = step(i * ratio + jj, carry, True)
    m, l, acc = carry
    lam = _diff_lambda(lq1_ref[...], lk1_ref[...], lq2_ref[...], lk2_ref[...])
    o = acc / l
    o = o[0:tq, :] - lam * o[tq:rows, :]
    o_ref[...] = _sub_norm(o, sg_ref[...]).astype(o_ref.dtype)


def _attn_prompt(qb, kb, vb, lam_p, subln_g, bsz, t, n_heads):
    m, aw = qb.shape
    hw = aw // n_heads
    tq = min(ATT_Q, t)
    tk = min(ATT_K, tq)
    assert t % tq == 0 and tq % tk == 0
    nq = t // tq
    vec = _const_spec((1, HEAD_DIM))
    return pl.pallas_call(
        functools.partial(_attn_prompt_kernel, tq=tq, tk=tk),
        grid=(bsz, n_heads, nq),
        in_specs=[
            pl.BlockSpec((tq, hw), lambda b, h, i: (b * nq + i, h)),
            pl.BlockSpec((t, hw), lambda b, h, i: (b, h)),
            pl.BlockSpec((t, hw), lambda b, h, i: (b, h)),
            vec, vec, vec, vec, _const_spec((1, hw)),
        ],
        out_specs=pl.BlockSpec((tq, hw), lambda b, h, i: (b * nq + i, h)),
        out_shape=jax.ShapeDtypeStruct((m, aw), BF16),
        compiler_params=_params("parallel", "parallel", "arbitrary"),
        name="attn_prompt",
    )(qb, kb, vb, *lam_p, subln_g)


def _attn_sample_kernel(pt_ref, q_ref, kn_ref, vn_ref, lq1_ref, lk1_ref, lq2_ref, lk2_ref, sg_ref,
                        *refs, n_pages, n_heads):
    k_refs = refs[:n_pages]
    v_refs = refs[n_pages:2 * n_pages]
    o_ref = refs[2 * n_pages]
    qm_ref, s_ref, m_ref, l_ref, acc_ref = refs[2 * n_pages + 1:]
    p_idx = pl.program_id(1)
    hw = 2 * HEAD_DIM
    row = lax.broadcasted_iota(jnp.int32, (SUBLANES, hw), 0)
    lane = lax.broadcasted_iota(jnp.int32, (SUBLANES, hw), 1)
    comp_mask = ((row == 0) & (lane < HEAD_DIM)) | ((row == 1) & (lane >= HEAD_DIM))

    @pl.when(p_idx == 0)
    def _():
        q = q_ref[...].astype(F32)
        for h in range(n_heads):
            qh = jnp.broadcast_to(q[:, h * hw:(h + 1) * hw], (SUBLANES, hw))
            qm_ref[h] = jnp.where(comp_mask, qh, jnp.zeros_like(qh))
        m_ref[...] = jnp.full(m_ref.shape, NEG_INF, F32)
        l_ref[...] = jnp.zeros(l_ref.shape, F32)
        acc_ref[...] = jnp.zeros(acc_ref.shape, F32)

    for pg in range(n_pages):
        k_ref = k_refs[pg]
        v_ref = v_refs[pg]
        for h in range(n_heads):
            kh = k_ref[:, h, :].astype(BF16)
            s_ref[h] = _dot_nt(qm_ref[h].astype(BF16), kh)
        s = s_ref[...]
        m_old = m_ref[...]
        m_new = jnp.maximum(m_old, jnp.max(s, axis=-1, keepdims=True))
        alpha = jnp.exp(m_old - m_new)
        p = jnp.exp(s - m_new)
        l_ref[...] = alpha * l_ref[...] + jnp.sum(p, axis=-1, keepdims=True)
        m_ref[...] = m_new
        for h in range(n_heads):
            vh = v_ref[:, h, :].astype(BF16)
            acc_ref[h] = alpha[h] * acc_ref[h] + _dot(p[h].astype(BF16), vh)

    @pl.when(p_idx == pl.num_programs(1) - 1)
    def _():
        lam = _diff_lambda(lq1_ref[...], lk1_ref[...], lq2_ref[...], lk2_ref[...])
        kn = kn_ref[...].astype(BF16).astype(F32)
        vn = vn_ref[...].astype(BF16).astype(F32)
        for h in range(n_heads):
            knh = kn[:, h * hw:(h + 1) * hw]
            vnh = vn[:, h * hw:(h + 1) * hw]
            s_new = jnp.sum(qm_ref[h] * knh, axis=-1, keepdims=True)
            m_old = m_ref[h]
            m_new = jnp.maximum(m_old, s_new)
            alpha = jnp.exp(m_old - m_new)
            p_new = jnp.exp(s_new - m_new)
            l = alpha * l_ref[h] + p_new
            acc = alpha * acc_ref[h] + p_new.astype(BF16).astype(F32) * vnh
            o = acc / l
            o = o[0:1, :] - lam * o[1:2, :]
            o_ref[:, h * hw:(h + 1) * hw] = _sub_norm(o, sg_ref[...]).astype(o_ref.dtype)


def _attn_sample(qb, k_new, v_new, lam_p, subln_g, cache_k, cache_v, page_table, n_heads):
    n, aw = qb.shape
    hw = aw // n_heads
    page = cache_k.shape[1]
    n_log = page_table.shape[1]
    pp = DEC_PAGES
    while n_log % pp:
        pp //= 2
    row3 = lambda a: a.reshape(n, 1, aw)
    row_spec = pl.BlockSpec((None, 1, aw), lambda b, p, pt: (b, 0, 0))
    vec = pl.BlockSpec((1, HEAD_DIM), lambda b, p, pt: (0, 0))

    def page_spec(j):
        return pl.BlockSpec((None, page, n_heads, hw), lambda b, p, pt: (pt[b, p * pp + j], 0, 0, 0))

    grid_spec = pltpu.PrefetchScalarGridSpec(
        num_scalar_prefetch=1,
        grid=(n, n_log // pp),
        in_specs=[row_spec, row_spec, row_spec, vec, vec, vec, vec,
                  pl.BlockSpec((1, hw), lambda b, p, pt: (0, 0))]
                 + [page_spec(j) for j in range(pp)] + [page_spec(j) for j in range(pp)],
        out_specs=row_spec,
        scratch_shapes=[
            pltpu.VMEM((n_heads, SUBLANES, hw), F32),
            pltpu.VMEM((n_heads, SUBLANES, page), F32),
            pltpu.VMEM((n_heads, SUBLANES, 1), F32),
            pltpu.VMEM((n_heads, SUBLANES, 1), F32),
            pltpu.VMEM((n_heads, SUBLANES, hw), F32),
        ],
    )
    out = pl.pallas_call(
        functools.partial(_attn_sample_kernel, n_pages=pp, n_heads=n_heads),
        grid_spec=grid_spec,
        out_shape=jax.ShapeDtypeStruct((n, 1, aw), BF16),
        compiler_params=_params("parallel", "arbitrary"),
        name="attn_sample",
    )(page_table, row3(qb), row3(k_new), row3(v_new), *lam_p, subln_g,
      *([cache_k] * pp), *([cache_v] * pp))
    return out.reshape(n, aw)


def _proj_ln_kernel(a_ref, x_ref, w_ref, g_ref, b_ref, o_ref):
    y = _dot(a_ref[...], w_ref[...])
    o_ref[...] = _layer_norm(ALPHA * x_ref[...] + y, g_ref[...], b_ref[...])


def _proj_ln(a, x, w, g, b):
    m, d = x.shape
    tm = min(PROJ_ROWS, m)
    return pl.pallas_call(
        _proj_ln_kernel,
        grid=(pl.cdiv(m, tm),),
        in_specs=[pl.BlockSpec((tm, a.shape[1]), lambda i: (i, 0)),
                  pl.BlockSpec((tm, d), lambda i: (i, 0)),
                  _const_spec(w.shape), _const_spec((1, d)), _const_spec((1, d))],
        out_specs=pl.BlockSpec((tm, d), lambda i: (i, 0)),
        out_shape=jax.ShapeDtypeStruct((m, d), F32),
        compiler_params=_params("parallel"),
        name="proj_ln",
    )(a, x, w, g, b)


def _block_diag(w):
    n, bi, bj = w.shape
    eye = jnp.eye(n, dtype=w.dtype)
    return jnp.einsum("nij,nm->nimj", w, eye).reshape(n * bi, n * bj)


def _chunk_cols(w):
    d, f = w.shape
    return jnp.swapaxes(w.reshape(d, f // FFN_CHUNK, FFN_CHUNK), 0, 1).astype(BF16)


def _chunk_rows(w):
    f, d = w.shape
    return w.reshape(f // FFN_CHUNK, FFN_CHUNK, d).astype(BF16)


def kernel(x_prompt, x_sample, state_lru_h, state_conv_a, state_conv_b, cache_k, cache_v, page_table,
           ln_g, ln_b, ffn_w_gate, ffn_w_up, ffn_w_down,
           rec_w_in, conv_a_w, conv_a_b, lru_w_a, lru_b_a, lru_w_x, lru_b_x, lru_lambda,
           conv_b_w, conv_b_b, conv_ln_g, conv_ln_b, rec_w_out,
           att_w_qkv, lambda_q1, lambda_k1, lambda_q2, lambda_k2, subln_g, att_w_out):
    bp, tp, d = x_prompt.shape
    bs, ts, _ = x_sample.shape
    assert ts == 1, "the sample group decodes one token per sequence"
    n_heads = cache_k.shape[2]
    d_a = state_lru_h.shape[-1]
    d_ff = ffn_w_gate.shape[-1]
    assert d_ff % FFN_CHUNK == 0
    row = lambda v: v.reshape(1, -1)

    ffn = [[(_chunk_cols(ffn_w_gate[l, i]), _chunk_cols(ffn_w_up[l, i]), _chunk_rows(ffn_w_down[l, i]))
            for i in range(2)] for l in range(DEPTH)]
    lng = [[row(ln_g[l, i]) for i in range(3)] for l in range(DEPTH)]
    lnb = [[row(ln_b[l, i]) for i in range(3)] for l in range(DEPTH)]
    mp = dict(
        w_in=rec_w_in.astype(BF16), conv_a_w=conv_a_w, conv_a_b=row(conv_a_b),
        wa=_block_diag(lru_w_a).astype(BF16), ba=row(lru_b_a),
        wx=_block_diag(lru_w_x).astype(BF16), bx=row(lru_b_x), lam=row(lru_lambda),
        conv_b_w=conv_b_w, conv_b_b=row(conv_b_b), cln_g=row(conv_ln_g), cln_b=row(conv_ln_b),
        w_out=rec_w_out.astype(BF16))
    w_qkv = att_w_qkv.astype(BF16)
    w_o = att_w_out.astype(BF16)
    lam_p = (row(lambda_q1), row(lambda_k1), row(lambda_q2), row(lambda_k2))
    sg = row(subln_g)

    def ffn_sub(x, l, i):
        return _ffn_sublayer(x, *ffn[l][i], lng[l][2 * i], lnb[l][2 * i])

    x = x_prompt.reshape(bp * tp, d)
    x = ffn_sub(x, 0, 0)
    x, h_p, ca_p, cb_p = _mixer_prompt(
        x.reshape(bp, tp, d), jnp.zeros((bp, d_a), F32),
        jnp.zeros((bp, state_conv_a.shape[1], d_a), F32),
        jnp.zeros((bp, state_conv_b.shape[1], d_a), F32), mp, lng[0][1], lnb[0][1])
    x = ffn_sub(x.reshape(bp * tp, d), 0, 1)
    x = ffn_sub(x, 1, 0)
    cos_p, sin_p = _rope_tables(jnp.arange(tp, dtype=jnp.int32))
    k_p, v_p, qb, kb, vb = _qkv_rope(x, w_qkv, cos_p, sin_p, n_heads, tp)
    att = _attn_prompt(qb, kb, vb, lam_p, sg, bp, tp, n_heads)
    x = _proj_ln(att, x, w_o, lng[1][1], lnb[1][1])
    y_prompt = ffn_sub(x, 1, 1).reshape(bp, tp, d)

    past = page_table.shape[1] * cache_k.shape[1]
    xs = x_sample.reshape(bs, d)
    xs = ffn_sub(xs, 0, 0)
    xs, h_s, ca_s, cb_s = _mixer_sample(xs, state_lru_h, state_conv_a, state_conv_b, mp,
                                        lng[0][1], lnb[0][1])
    xs = ffn_sub(xs, 0, 1)
    xs = ffn_sub(xs, 1, 0)
    cos_s, sin_s = _rope_tables(past + jnp.arange(ts, dtype=jnp.int32))
    k_s, v_s, qsb, _, _ = _qkv_rope(xs, w_qkv, cos_s, sin_s, n_heads, ts)
    att_s = _attn_sample(qsb, k_s, v_s, lam_p, sg, cache_k, cache_v, page_table, n_heads)
    xs = _proj_ln(att_s, xs, w_o, lng[1][1], lnb[1][1])
    y_sample = ffn_sub(xs, 1, 1).reshape(bs, ts, d)

    hw = 2 * HEAD_DIM
    return (y_prompt, y_sample, h_p, ca_p, cb_p,
            k_p.reshape(bp, tp, n_heads, hw), v_p.reshape(bp, tp, n_heads, hw),
            h_s, ca_s, cb_s,
            k_s.reshape(bs, ts, n_heads, hw), v_s.reshape(bs, ts, n_heads, hw))
```

```python
import functools
import math

import jax
import jax.numpy as jnp
from jax import lax
from jax.experimental import pallas as pl
from jax.experimental.pallas import tpu as pltpu

F32 = jnp.float32
BF16 = jnp.bfloat16

DEPTH = 2
LRU_BLOCKS = 8
LRU_C = 8.0
HEAD_DIM = 64
ROPE_THETA = 10000.0
ATT_SCALE = HEAD_DIM ** -0.5
LOG2_E = math.log2(math.e)
NEG_INF = -1e30
ALPHA = (2 * DEPTH) ** 0.25
LN_EPS = 1e-5
LAMBDA_INIT = 0.8 - 0.6 * math.exp(-0.3 * 1)

LANES = 128
SUBLANES = 8
VMEM_LIMIT_BYTES = 56 * 1024 * 1024

FFN_CHUNK = 256
FFN_ROWS = 512
MIX_ROWS = 256
QKV_ROWS = 512
ATT_TILE = 512
ATT_HEADS = 2
ATT_QCHUNK = 512
PROJ_ROWS = 512
DEC_PAGES = 8


def _layer_norm(y, g, b):
    mu = jnp.mean(y, axis=-1, keepdims=True)
    d = y - mu
    var = jnp.mean(d * d, axis=-1, keepdims=True)
    return d * lax.rsqrt(var + LN_EPS) * g + b


def _sigmoid(x):
    return 1.0 / (1.0 + jnp.exp(-x))


def _dot(a, b):
    return jnp.dot(a, b, preferred_element_type=F32)


def _dot_nt(a, b):
    return lax.dot_general(a, b, (((1,), (1,)), ((), ())), preferred_element_type=F32)


def _const_spec(shape):
    nd = len(shape)
    return pl.BlockSpec(shape, lambda *_: (0,) * nd)


def _params(*sem):
    return pltpu.CompilerParams(dimension_semantics=sem, vmem_limit_bytes=VMEM_LIMIT_BYTES)


def _ffn_kernel(x_ref, wg_ref, wu_ref, wd_ref, g_ref, b_ref, o_ref, *, n_chunks):
    x = x_ref[...]
    xb = x.astype(BF16)
    acc = None
    for c in range(n_chunks):
        g = _dot(xb, wg_ref[c])
        u = _dot(xb, wu_ref[c])
        a = (g * _sigmoid(g) * u).astype(BF16)
        d = _dot(a, wd_ref[c])
        acc = d if acc is None else acc + d
    o_ref[...] = _layer_norm(ALPHA * x + 0.5 * acc, g_ref[...], b_ref[...])


def _ffn_sublayer(x, wg, wu, wd, g, b):
    m, d = x.shape
    n_chunks, _, fc = wg.shape
    tm = min(FFN_ROWS, m)
    return pl.pallas_call(
        functools.partial(_ffn_kernel, n_chunks=n_chunks),
        grid=(pl.cdiv(m, tm),),
        in_specs=[
            pl.BlockSpec((tm, d), lambda i: (i, 0)),
            _const_spec((n_chunks, d, fc)),
            _const_spec((n_chunks, d, fc)),
            _const_spec((n_chunks, fc, d)),
            _const_spec((1, d)),
            _const_spec((1, d)),
        ],
        out_specs=pl.BlockSpec((tm, d), lambda i: (i, 0)),
        out_shape=jax.ShapeDtypeStruct((m, d), F32),
        compiler_params=_params("parallel"),
        name="ffn_sublayer",
    )(x, wg, wu, wd, g, b)


def _softplus(x):
    return jnp.maximum(x, 0.0) + jnp.log1p(jnp.exp(-jnp.abs(x)))


def _gelu_tanh(x):
    c = math.sqrt(2.0 / math.pi)
    return 0.5 * x * (1.0 + jnp.tanh(c * (x + 0.044715 * (x * x * x))))


def _lru_gates(xa_c, wa, ba, wx, bx, lam):
    xb = xa_c.astype(BF16)
    r = _sigmoid(_dot(xb, wa) + ba)
    i = _sigmoid(_dot(xb, wx) + bx)
    log_a = (-LRU_C) * r * _softplus(-lam)
    a = jnp.exp(log_a)
    th = jnp.tanh(log_a)
    u = jnp.sqrt(-2.0 * th / (1.0 - th)) * (i * xa_c)
    return a, u


def _conv_b_post(acc, g, b):
    y = _layer_norm(acc, g, b)
    return y * _sigmoid(y)


HIST_A = 8
HIST_B = 32


def _mixer_prompt_kernel(x_ref, h0_ref, ca0_ref, cb0_ref, w_in_ref, caw_ref, cab_ref,
                         wa_ref, ba_ref, wx_ref, bx_ref, lam_ref, cbw_ref, cbb_ref,
                         clg_ref, clb_ref, wout_ref, g_ref, b_ref,
                         y_ref, hl_ref, cas_ref, cbs_ref,
                         ha_ref, hb_ref, sh_ref, sa_ref, su_ref, hc_ref, *, tm, pad, ka, kb):
    d_a = ha_ref.shape[1]
    t = pl.program_id(1)

    @pl.when(t == 0)
    def _():
        ha_ref[HIST_A - (ka - 1):HIST_A, :] = ca0_ref[...]
        hb_ref[HIST_B - (kb - 1):HIST_B, :] = cb0_ref[...]
        hc_ref[...] = h0_ref[...]
        sa_ref[0:pad, :] = jnp.ones((pad, d_a), F32)
        su_ref[0:pad, :] = jnp.zeros((pad, d_a), F32)

    x = x_ref[...]
    z = _dot(x.astype(BF16), w_in_ref[...])
    gate_a = z[:, 0:d_a]
    xa = z[:, d_a:2 * d_a]
    vb = z[:, 2 * d_a:3 * d_a]
    gb = z[:, 3 * d_a:4 * d_a]

    ha_ref[HIST_A:HIST_A + tm, :] = xa
    xa_c = cab_ref[...]
    for j in range(ka):
        s = HIST_A - (ka - 1) + j
        xa_c = xa_c + caw_ref[j:j + 1, :] * ha_ref[s:s + tm, :]
    a, u = _lru_gates(xa_c, wa_ref[...], ba_ref[...], wx_ref[...], bx_ref[...], lam_ref[...])
    sa_ref[pad:pad + tm, :] = a
    su_ref[pad:pad + tm, :] = u
    su_ref[pad:pad + 1, :] = u[0:1, :] + a[0:1, :] * hc_ref[...]
    k = 1
    while k < tm:
        a_cur = sa_ref[pad:pad + tm, :]
        u_cur = su_ref[pad:pad + tm, :]
        a_sh = sa_ref[pad - k:pad - k + tm, :]
        u_sh = su_ref[pad - k:pad - k + tm, :]
        su_ref[pad:pad + tm, :] = u_cur + a_cur * u_sh
        if 2 * k < tm:
            sa_ref[pad:pad + tm, :] = a_cur * a_sh
        k *= 2
    h_seq = su_ref[pad:pad + tm, :]
    h_last = h_seq[tm - 1:tm, :]
    hc_ref[...] = h_last
    hl_ref[...] = h_last
    ya = h_seq * _gelu_tanh(gate_a)

    hb_ref[HIST_B:HIST_B + tm, :] = vb * _sigmoid(gb)
    span = HIST_B - SUBLANES + tm
    acc = cbb_ref[...]
    for r in range(SUBLANES):
        taps = [j for j in range(kb) if (HIST_B - (kb - 1) + j) % SUBLANES == r]
        if not taps:
            continue
        if r:
            sh_ref[r - 1] = hb_ref[r:r + span, :]
        for j in taps:
            s = HIST_B - (kb - 1) + j - r
            rows = sh_ref[r - 1, s:s + tm, :] if r else hb_ref[s:s + tm, :]
            acc = acc + cbw_ref[j:j + 1, :] * rows
    yb = _conv_b_post(acc, clg_ref[...], clb_ref[...])

    m = _dot(ya.astype(BF16), wout_ref[0:d_a, :]) + _dot(yb.astype(BF16), wout_ref[d_a:2 * d_a, :])
    y_ref[...] = _layer_norm(ALPHA * x + m, g_ref[...], b_ref[...])

    tail_a = ha_ref[tm + HIST_A - (ka - 1):tm + HIST_A, :]
    cas_ref[...] = tail_a
    ha_ref[HIST_A - (ka - 1):HIST_A, :] = tail_a
    tail_b = hb_ref[tm + HIST_B - (kb - 1):tm + HIST_B, :]
    cbs_ref[...] = tail_b
    hb_ref[HIST_B - (kb - 1):HIST_B, :] = tail_b


def _mixer_prompt(x, h0, ca0, cb0, mp, g, b):
    bsz, t, d = x.shape
    d_a = h0.shape[-1]
    ka = ca0.shape[1] + 1
    kb = cb0.shape[1] + 1
    assert ka - 1 <= HIST_A and kb - 1 <= HIST_B
    tm = min(MIX_ROWS, t)
    assert t % tm == 0
    pad = max(SUBLANES, tm // 2)
    kernel = functools.partial(_mixer_prompt_kernel, tm=tm, pad=pad, ka=ka, kb=kb)
    bt = lambda bi, ti: (bi, ti, 0)
    b0 = lambda bi, ti: (bi, 0, 0)
    outs = pl.pallas_call(
        kernel,
        grid=(bsz, t // tm),
        in_specs=[
            pl.BlockSpec((None, tm, d), bt),
            pl.BlockSpec((None, 1, d_a), b0),
            pl.BlockSpec((None, ka - 1, d_a), b0),
            pl.BlockSpec((None, kb - 1, d_a), b0),
            _const_spec(mp["w_in"].shape),
            _const_spec((ka, d_a)), _const_spec((1, d_a)),
            _const_spec((d_a, d_a)), _const_spec((1, d_a)),
            _const_spec((d_a, d_a)), _const_spec((1, d_a)),
            _const_spec((1, d_a)),
            _const_spec((kb, d_a)), _const_spec((1, d_a)),
            _const_spec((1, d_a)), _const_spec((1, d_a)),
            _const_spec(mp["w_out"].shape),
            _const_spec((1, d)), _const_spec((1, d)),
        ],
        out_specs=[
            pl.BlockSpec((None, tm, d), bt),
            pl.BlockSpec((None, 1, d_a), b0),
            pl.BlockSpec((None, ka - 1, d_a), b0),
            pl.BlockSpec((None, kb - 1, d_a), b0),
        ],
        out_shape=[
            jax.ShapeDtypeStruct((bsz, t, d), F32),
            jax.ShapeDtypeStruct((bsz, 1, d_a), F32),
            jax.ShapeDtypeStruct((bsz, ka - 1, d_a), F32),
            jax.ShapeDtypeStruct((bsz, kb - 1, d_a), F32),
        ],
        scratch_shapes=[
            pltpu.VMEM((HIST_A + tm, d_a), F32),
            pltpu.VMEM((HIST_B + tm, d_a), F32),
            pltpu.VMEM((SUBLANES - 1, HIST_B - SUBLANES + tm, d_a), F32),
            pltpu.VMEM((pad + tm, d_a), F32),
            pltpu.VMEM((pad + tm, d_a), F32),
            pltpu.VMEM((1, d_a), F32),
        ],
        compiler_params=_params("parallel", "arbitrary"),
        name="mixer_prompt",
    )(x, h0.reshape(bsz, 1, d_a), ca0, cb0, mp["w_in"], mp["conv_a_w"], mp["conv_a_b"],
      mp["wa"], mp["ba"], mp["wx"], mp["bx"], mp["lam"], mp["conv_b_w"], mp["conv_b_b"],
      mp["cln_g"], mp["cln_b"], mp["w_out"], g, b)
    y, hl, cas, cbs = outs
    return y, hl.reshape(bsz, d_a), cas, cbs


def _mixer_sample_kernel(x_ref, h0_ref, ca_ref, cb_ref, w_in_ref, caw_ref, cab_ref,
                         wa_ref, ba_ref, wx_ref, bx_ref, lam_ref, cbw_ref, cbb_ref,
                         clg_ref, clb_ref, wout_ref, g_ref, b_ref,
                         y_ref, hl_ref, cas_ref, cbs_ref, *, ka, kb):
    d_a = h0_ref.shape[1]
    x = x_ref[...]
    z = _dot(x.astype(BF16), w_in_ref[...])
    gate_a = z[:, 0:d_a]
    xa = z[:, d_a:2 * d_a]
    vb = z[:, 2 * d_a:3 * d_a]
    gb = z[:, 3 * d_a:4 * d_a]

    xa_c = cab_ref[...]
    for j in range(ka - 1):
        xa_c = xa_c + caw_ref[j:j + 1, :] * ca_ref[j]
    xa_c = xa_c + caw_ref[ka - 1:ka, :] * xa
    a, u = _lru_gates(xa_c, wa_ref[...], ba_ref[...], wx_ref[...], bx_ref[...], lam_ref[...])
    h = a * h0_ref[...] + u
    hl_ref[...] = h
    ya = h * _gelu_tanh(gate_a)

    ub = vb * _sigmoid(gb)
    acc = cbb_ref[...]
    for j in range(kb - 1):
        acc = acc + cbw_ref[j:j + 1, :] * cb_ref[j]
    acc = acc + cbw_ref[kb - 1:kb, :] * ub
    yb = _conv_b_post(acc, clg_ref[...], clb_ref[...])

    m = _dot(ya.astype(BF16), wout_ref[0:d_a, :]) + _dot(yb.astype(BF16), wout_ref[d_a:2 * d_a, :])
    y_ref[...] = _layer_norm(ALPHA * x + m, g_ref[...], b_ref[...])

    for j in range(ka - 2):
        cas_ref[j] = ca_ref[j + 1]
    cas_ref[ka - 2] = xa
    for j in range(kb - 2):
        cbs_ref[j] = cb_ref[j + 1]
    cbs_ref[kb - 2] = ub


def _mixer_sample(x, h0, ca0, cb0, mp, g, b):
    n, d = x.shape
    d_a = h0.shape[-1]
    ka = ca0.shape[1] + 1
    kb = cb0.shape[1] + 1
    ca_t = jnp.swapaxes(ca0, 0, 1)
    cb_t = jnp.swapaxes(cb0, 0, 1)
    args = (x, h0, ca_t, cb_t, mp["w_in"], mp["conv_a_w"], mp["conv_a_b"],
            mp["wa"], mp["ba"], mp["wx"], mp["bx"], mp["lam"], mp["conv_b_w"], mp["conv_b_b"],
            mp["cln_g"], mp["cln_b"], mp["w_out"], g, b)
    y, hl, cas, cbs = pl.pallas_call(
        functools.partial(_mixer_sample_kernel, ka=ka, kb=kb),
        grid=(1,),
        in_specs=[_const_spec(a.shape) for a in args],
        out_specs=[_const_spec((n, d)), _const_spec((n, d_a)),
                   _const_spec((ka - 1, n, d_a)), _const_spec((kb - 1, n, d_a))],
        out_shape=[
            jax.ShapeDtypeStruct((n, d), F32),
            jax.ShapeDtypeStruct((n, d_a), F32),
            jax.ShapeDtypeStruct((ka - 1, n, d_a), F32),
            jax.ShapeDtypeStruct((kb - 1, n, d_a), F32),
        ],
        compiler_params=_params("arbitrary"),
        name="mixer_sample",
    )(*args)
    return y, hl, jnp.swapaxes(cas, 0, 1), jnp.swapaxes(cbs, 0, 1)


def _rope_tables(pos):
    half = HEAD_DIM // 2
    inv = ROPE_THETA ** (-jnp.arange(half, dtype=F32) / half)
    ang = pos.astype(F32)[:, None] * inv[None, :]
    cos = jnp.cos(ang)
    sin = jnp.sin(ang)
    c = jnp.concatenate([cos, cos, cos, cos], axis=-1)
    s = jnp.concatenate([-sin, sin, -sin, sin], axis=-1)
    return c, s


def _qkv_rope_kernel(x_ref, w_ref, cos_ref, sin_ref, k_ref, v_ref, qb_ref, *tile_refs,
                     n_heads, att_tile):
    aw = n_heads * 2 * HEAD_DIM
    z = _dot(x_ref[...].astype(BF16), w_ref[...])
    c = cos_ref[...]
    s = sin_ref[...]
    hw = 2 * HEAD_DIM
    lane = lax.broadcasted_iota(jnp.int32, (1, hw), 1)
    first_half = (lane % HEAD_DIM) < (HEAD_DIM // 2)

    def rot(xh):
        swapped = jnp.where(first_half, pltpu.roll(xh, hw - HEAD_DIM // 2, axis=1),
                            pltpu.roll(xh, HEAD_DIM // 2, axis=1))
        return xh * c + swapped * s

    for h in range(n_heads):
        sl = slice(h * hw, (h + 1) * hw)
        qh = rot(z[:, h * hw:(h + 1) * hw])
        kh = rot(z[:, aw + h * hw:aw + (h + 1) * hw])
        vh = z[:, 2 * aw + h * hw:2 * aw + (h + 1) * hw]
        qb_ref[:, sl] = (qh * (ATT_SCALE * LOG2_E)).astype(BF16)
        k_ref[:, sl] = kh
        v_ref[:, sl] = vh
        if att_tile is not None:
            kb_ref, vt_ref = tile_refs
            kb_ref[:, sl] = kh.astype(BF16)
            for ti in range(vt_ref.shape[0]):
                vt_ref[ti, h] = vh[ti * att_tile:(ti + 1) * att_tile, :].T.astype(BF16)


def _qkv_rope(x, w_qkv, cos, sin, n_heads, t_per_seq, att_tile=None):
    m, d = x.shape
    aw = n_heads * 2 * HEAD_DIM
    hw = 2 * HEAD_DIM
    if t_per_seq == 1:
        tm = m
        tab_spec = pl.BlockSpec((1, hw), lambda i: (0, 0))
    else:
        tm = min(QKV_ROWS, t_per_seq)
        assert t_per_seq % tm == 0
        nt = t_per_seq // tm
        tab_spec = pl.BlockSpec((tm, hw), lambda i: (i % nt, 0))
    row = pl.BlockSpec((tm, aw), lambda i: (i, 0))
    out_specs = [row, row, row]
    out_shape = [jax.ShapeDtypeStruct((m, aw), F32), jax.ShapeDtypeStruct((m, aw), F32),
                 jax.ShapeDtypeStruct((m, aw), BF16)]
    if att_tile is not None:
        assert tm % att_tile == 0
        tiles = tm // att_tile
        out_specs += [row, pl.BlockSpec((tiles, n_heads, hw, att_tile), lambda i: (i, 0, 0, 0))]
        out_shape += [jax.ShapeDtypeStruct((m, aw), BF16),
                      jax.ShapeDtypeStruct((m // att_tile, n_heads, hw, att_tile), BF16)]
    return pl.pallas_call(
        functools.partial(_qkv_rope_kernel, n_heads=n_heads, att_tile=att_tile),
        grid=(m // tm,),
        in_specs=[pl.BlockSpec((tm, d), lambda i: (i, 0)), _const_spec(w_qkv.shape),
                  tab_spec, tab_spec],
        out_specs=out_specs,
        out_shape=out_shape,
        compiler_params=_params("parallel"),
        name="qkv_rope",
    )(x, w_qkv, cos, sin)


def _diff_lambda(lq1, lk1, lq2, lk2):
    s1 = jnp.sum(lq1 * lk1, axis=-1, keepdims=True)
    s2 = jnp.sum(lq2 * lk2, axis=-1, keepdims=True)
    return jnp.exp(s1) - jnp.exp(s2) + LAMBDA_INIT


def _sub_norm(o, g):
    return o * lax.rsqrt(jnp.mean(o * o, axis=-1, keepdims=True) + LN_EPS) * g * (1.0 - LAMBDA_INIT)


def _attn_prompt_kernel(q_ref, k_ref, vt_ref, lq1_ref, lk1_ref, lq2_ref, lk2_ref, sg_ref, o_ref,
                        q2_ref, s_ref, m_ref, l_ref, acc_ref, *, t, heads, qc):
    i = pl.program_id(2)
    hw = 2 * HEAD_DIM
    lane = lax.broadcasted_iota(jnp.int32, (t, hw), 1)
    for g in range(heads):
        q = q_ref[:, g * hw:(g + 1) * hw]
        zero = jnp.zeros_like(q)
        q2_ref[g, 0:t, :] = jnp.where(lane < HEAD_DIM, q, zero)
        q2_ref[g, t:2 * t, :] = jnp.where(lane >= HEAD_DIM, q, zero)
        m_ref[g] = jnp.full((1, 2 * t), NEG_INF, F32)
        l_ref[g] = jnp.zeros((1, 2 * t), F32)
        acc_ref[g] = jnp.zeros((hw, 2 * t), F32)

    chains = [(g, c) for g in range(heads) for c in range(2 * t // qc)]
    assert len(chains) % 2 == 0

    def scores(off, g, c):
        kt = k_ref[pl.ds(off, t), g * hw:(g + 1) * hw]
        return _dot_nt(kt, q2_ref[g, c * qc:(c + 1) * qc, :])

    def step(j, masked):
        off = pl.multiple_of(j * t, t)
        for idx, (g, c) in enumerate(chains):
            if idx + 1 < len(chains):
                s_ref[(idx + 1) % 2] = scores(off, *chains[idx + 1])
            elif not masked:
                s_ref[0] = scores(pl.multiple_of(off + t, t), *chains[0])
            cs = slice(c * qc, (c + 1) * qc)
            s = s_ref[idx % 2]
            if masked:
                key_pos = lax.broadcasted_iota(jnp.int32, (t, qc), 0)
                qry_pos = lax.broadcasted_iota(jnp.int32, (t, qc), 1) + (c * qc) % t
                s = jnp.where(key_pos <= qry_pos, s, NEG_INF)
            m_old = m_ref[g, :, cs]
            m_new = jnp.maximum(m_old, jnp.max(s, axis=0, keepdims=True))
            alpha = jnp.exp2(m_old - m_new)
            p = jnp.exp2(s - m_new)
            l_ref[g, :, cs] = alpha * l_ref[g, :, cs] + jnp.sum(p, axis=0, keepdims=True)
            m_ref[g, :, cs] = m_new
            acc_ref[g, :, cs] = alpha * acc_ref[g, :, cs] + _dot(vt_ref[j, g], p.astype(BF16))

    def body(j, carry):
        step(j, False)
        return carry

    s_ref[0] = scores(0, *chains[0])
    lax.fori_loop(0, i, body, 0)
    step(i, True)

    lam = _diff_lambda(lq1_ref[...], lk1_ref[...], lq2_ref[...], lk2_ref[...])
    for g in range(heads):
        a = acc_ref[g] * (1.0 / l_ref[g])
        o = (a[:, 0:t] - lam * a[:, t:2 * t]).T
        o_ref[:, g * hw:(g + 1) * hw] = _sub_norm(o, sg_ref[...]).astype(o_ref.dtype)


def _attn_prompt(qb, kb, vt, lam_p, subln_g, bsz, t_seq, n_heads, t):
    m, aw = qb.shape
    hw = aw // n_heads
    heads = min(ATT_HEADS, n_heads)
    assert t_seq % t == 0 and n_heads % heads == 0
    nq = t_seq // t
    vec = _const_spec((1, HEAD_DIM))
    qc = math.gcd(ATT_QCHUNK, t)
    return pl.pallas_call(
        functools.partial(_attn_prompt_kernel, t=t, heads=heads, qc=qc),
        grid=(bsz, n_heads // heads, nq),
        in_specs=[
            pl.BlockSpec((t, heads * hw), lambda b, h, i: (b * nq + i, h)),
            pl.BlockSpec((t_seq, heads * hw), lambda b, h, i: (b, h)),
            pl.BlockSpec((nq, heads, hw, t), lambda b, h, i: (b, h, 0, 0)),
            vec, vec, vec, vec, _const_spec((1, hw)),
        ],
        out_specs=pl.BlockSpec((t, heads * hw), lambda b, h, i: (b * nq + i, h)),
        out_shape=jax.ShapeDtypeStruct((m, aw), BF16),
        scratch_shapes=[
            pltpu.VMEM((heads, 2 * t, hw), BF16),
            pltpu.VMEM((2, t, qc), F32),
            pltpu.VMEM((heads, 1, 2 * t), F32),
            pltpu.VMEM((heads, 1, 2 * t), F32),
            pltpu.VMEM((heads, hw, 2 * t), F32),
        ],
        compiler_params=_params("parallel", "parallel", "arbitrary"),
        name="attn_prompt",
    )(qb, kb, vt, *lam_p, subln_g)


def _attn_sample_kernel(pt_ref, q_ref, kn_ref, vn_ref, lq1_ref, lk1_ref, lq2_ref, lk2_ref, sg_ref,
                        *refs, n_pages, n_heads, page):
    k_refs = refs[:n_pages]
    v_refs = refs[n_pages:2 * n_pages]
    o_ref = refs[2 * n_pages]
    qm_ref, s_ref, m_ref, l_ref, acc_ref = refs[2 * n_pages + 1:]
    p_idx = pl.program_id(1)
    hw = 2 * HEAD_DIM
    row = lax.broadcasted_iota(jnp.int32, (SUBLANES, hw), 0)
    lane = lax.broadcasted_iota(jnp.int32, (SUBLANES, hw), 1)
    comp_mask = ((row == 0) & (lane < HEAD_DIM)) | ((row == 1) & (lane >= HEAD_DIM))

    @pl.when(p_idx == 0)
    def _():
        q = q_ref[...].astype(F32)
        for h in range(n_heads):
            qh = jnp.broadcast_to(q[:, h * hw:(h + 1) * hw], (SUBLANES, hw))
            qm_ref[h] = jnp.where(comp_mask, qh, jnp.zeros_like(qh))
        m_ref[...] = jnp.full(m_ref.shape, NEG_INF, F32)
        l_ref[...] = jnp.zeros(l_ref.shape, F32)
        acc_ref[...] = jnp.zeros(acc_ref.shape, F32)

    def head_rows(page_refs, h):
        parts = [r[pl.ds(h, page, stride=n_heads), :] for r in page_refs]
        return jnp.concatenate(parts, axis=0).astype(BF16)

    for h in range(n_heads):
        s_ref[h] = _dot_nt(qm_ref[h].astype(BF16), head_rows(k_refs, h))
    s = s_ref[...]
    m_old = m_ref[...]
    m_new = jnp.maximum(m_old, jnp.max(s, axis=-1, keepdims=True))
    alpha = jnp.exp2(m_old - m_new)
    p = jnp.exp2(s - m_new)
    l_ref[...] = alpha * l_ref[...] + jnp.sum(p, axis=-1, keepdims=True)
    m_ref[...] = m_new
    for h in range(n_heads):
        acc_ref[h] = alpha[h] * acc_ref[h] + _dot(p[h].astype(BF16), head_rows(v_refs, h))

    @pl.when(p_idx == pl.num_programs(1) - 1)
    def _():
        lam = _diff_lambda(lq1_ref[...], lk1_ref[...], lq2_ref[...], lk2_ref[...])
        kn = kn_ref[...].astype(BF16).astype(F32)
        vn = vn_ref[...].astype(BF16).astype(F32)
        for h in range(n_heads):
            knh = kn[:, h * hw:(h + 1) * hw]
            vnh = vn[:, h * hw:(h + 1) * hw]
            s_new = jnp.sum(qm_ref[h] * knh, axis=-1, keepdims=True)
            m_old = m_ref[h]
            m_new = jnp.maximum(m_old, s_new)
            alpha = jnp.exp2(m_old - m_new)
            p_new = jnp.exp2(s_new - m_new)
            l = alpha * l_ref[h] + p_new
            acc = alpha * acc_ref[h] + p_new.astype(BF16).astype(F32) * vnh
            o = acc / l
            o = o[0:1, :] - lam * o[1:2, :]
            o_ref[:, h * hw:(h + 1) * hw] = _sub_norm(o, sg_ref[...]).astype(o_ref.dtype)


def _attn_sample(qb, k_new, v_new, lam_p, subln_g, cache_k, cache_v, page_table, n_heads):
    n, aw = qb.shape
    hw = aw // n_heads
    page = cache_k.shape[1]
    n_log = page_table.shape[1]
    pp = DEC_PAGES
    while n_log % pp:
        pp //= 2
    row3 = lambda a: a.reshape(n, 1, aw)
    row_spec = pl.BlockSpec((None, 1, aw), lambda b, p, pt: (b, 0, 0))
    vec = pl.BlockSpec((1, HEAD_DIM), lambda b, p, pt: (0, 0))

    def page_spec(j):
        return pl.BlockSpec((None, page * n_heads, hw), lambda b, p, pt: (pt[b, p * pp + j], 0, 0))

    grid_spec = pltpu.PrefetchScalarGridSpec(
        num_scalar_prefetch=1,
        grid=(n, n_log // pp),
        in_specs=[row_spec, row_spec, row_spec, vec, vec, vec, vec,
                  pl.BlockSpec((1, hw), lambda b, p, pt: (0, 0))]
                 + [page_spec(j) for j in range(pp)] + [page_spec(j) for j in range(pp)],
        out_specs=row_spec,
        scratch_shapes=[
            pltpu.VMEM((n_heads, SUBLANES, hw), F32),
            pltpu.VMEM((n_heads, SUBLANES, pp * page), F32),
            pltpu.VMEM((n_heads, SUBLANES, 1), F32),
            pltpu.VMEM((n_heads, SUBLANES, 1), F32),
            pltpu.VMEM((n_heads, SUBLANES, hw), F32),
        ],
    )
    ck = cache_k.reshape(cache_k.shape[0], page * n_heads, hw)
    cv = cache_v.reshape(cache_v.shape[0], page * n_heads, hw)
    out = pl.pallas_call(
        functools.partial(_attn_sample_kernel, n_pages=pp, n_heads=n_heads, page=page),
        grid_spec=grid_spec,
        out_shape=jax.ShapeDtypeStruct((n, 1, aw), BF16),
        compiler_params=_params("parallel", "arbitrary"),
        name="attn_sample",
    )(page_table, row3(qb), row3(k_new), row3(v_new), *lam_p, subln_g,
      *([ck] * pp), *([cv] * pp))
    return out.reshape(n, aw)


def _proj_ln_kernel(a_ref, x_ref, w_ref, g_ref, b_ref, o_ref):
    y = _dot(a_ref[...], w_ref[...])
    o_ref[...] = _layer_norm(ALPHA * x_ref[...] + y, g_ref[...], b_ref[...])


def _proj_ln(a, x, w, g, b):
    m, d = x.shape
    tm = min(PROJ_ROWS, m)
    return pl.pallas_call(
        _proj_ln_kernel,
        grid=(pl.cdiv(m, tm),),
        in_specs=[pl.BlockSpec((tm, a.shape[1]), lambda i: (i, 0)),
                  pl.BlockSpec((tm, d), lambda i: (i, 0)),
                  _const_spec(w.shape), _const_spec((1, d)), _const_spec((1, d))],
        out_specs=pl.BlockSpec((tm, d), lambda i: (i, 0)),
        out_shape=jax.ShapeDtypeStruct((m, d), F32),
        compiler_params=_params("parallel"),
        name="proj_ln",
    )(a, x, w, g, b)


def _block_diag(w):
    n, bi, bj = w.shape
    eye = jnp.eye(n, dtype=w.dtype)
    return jnp.einsum("nij,nm->nimj", w, eye).reshape(n * bi, n * bj)


def _chunk_cols(w):
    d, f = w.shape
    return jnp.swapaxes(w.reshape(d, f // FFN_CHUNK, FFN_CHUNK), 0, 1).astype(BF16)


def _chunk_rows(w):
    f, d = w.shape
    return w.reshape(f // FFN_CHUNK, FFN_CHUNK, d).astype(BF16)


def kernel(x_prompt, x_sample, state_lru_h, state_conv_a, state_conv_b, cache_k, cache_v, page_table,
           ln_g, ln_b, ffn_w_gate, ffn_w_up, ffn_w_down,
           rec_w_in, conv_a_w, conv_a_b, lru_w_a, lru_b_a, lru_w_x, lru_b_x, lru_lambda,
           conv_b_w, conv_b_b, conv_ln_g, conv_ln_b, rec_w_out,
           att_w_qkv, lambda_q1, lambda_k1, lambda_q2, lambda_k2, subln_g, att_w_out):
    bp, tp, d = x_prompt.shape
    bs, ts, _ = x_sample.shape
    assert ts == 1, "the sample group decodes one token per sequence"
    n_heads = cache_k.shape[2]
    d_a = state_lru_h.shape[-1]
    d_ff = ffn_w_gate.shape[-1]
    assert d_ff % FFN_CHUNK == 0
    row = lambda v: v.reshape(1, -1)

    ffn = [[(_chunk_cols(ffn_w_gate[l, i]), _chunk_cols(ffn_w_up[l, i]), _chunk_rows(ffn_w_down[l, i]))
            for i in range(2)] for l in range(DEPTH)]
    lng = [[row(ln_g[l, i]) for i in range(3)] for l in range(DEPTH)]
    lnb = [[row(ln_b[l, i]) for i in range(3)] for l in range(DEPTH)]
    mp = dict(
        w_in=rec_w_in.astype(BF16), conv_a_w=conv_a_w, conv_a_b=row(conv_a_b),
        wa=_block_diag(lru_w_a).astype(BF16), ba=row(lru_b_a),
        wx=_block_diag(lru_w_x).astype(BF16), bx=row(lru_b_x), lam=row(lru_lambda),
        conv_b_w=conv_b_w, conv_b_b=row(conv_b_b), cln_g=row(conv_ln_g), cln_b=row(conv_ln_b),
        w_out=rec_w_out.astype(BF16))
    w_qkv = att_w_qkv.astype(BF16)
    w_o = att_w_out.astype(BF16)
    lam_p = (row(lambda_q1), row(lambda_k1), row(lambda_q2), row(lambda_k2))
    sg = row(subln_g)

    def ffn_sub(x, l, i):
        return _ffn_sublayer(x, *ffn[l][i], lng[l][2 * i], lnb[l][2 * i])

    x = x_prompt.reshape(bp * tp, d)
    x = ffn_sub(x, 0, 0)
    x, h_p, ca_p, cb_p = _mixer_prompt(
        x.reshape(bp, tp, d), jnp.zeros((bp, d_a), F32),
        jnp.zeros((bp, state_conv_a.shape[1], d_a), F32),
        jnp.zeros((bp, state_conv_b.shape[1], d_a), F32), mp, lng[0][1], lnb[0][1])
    x = ffn_sub(x.reshape(bp * tp, d), 0, 1)
    x = ffn_sub(x, 1, 0)
    cos_p, sin_p = _rope_tables(jnp.arange(tp, dtype=jnp.int32))
    att_tile = min(ATT_TILE, tp)
    k_p, v_p, qb, kb, vt = _qkv_rope(x, w_qkv, cos_p, sin_p, n_heads, tp, att_tile)
    att = _attn_prompt(qb, kb, vt, lam_p, sg, bp, tp, n_heads, att_tile)
    x = _proj_ln(att, x, w_o, lng[1][1], lnb[1][1])
    y_prompt = ffn_sub(x, 1, 1).reshape(bp, tp, d)

    past = page_table.shape[1] * cache_k.shape[1]
    xs = x_sample.reshape(bs, d)
    xs = ffn_sub(xs, 0, 0)
    xs, h_s, ca_s, cb_s = _mixer_sample(xs, state_lru_h, state_conv_a, state_conv_b, mp,
                                        lng[0][1], lnb[0][1])
    xs = ffn_sub(xs, 0, 1)
    xs = ffn_sub(xs, 1, 0)
    cos_s, sin_s = _rope_tables(past + jnp.arange(ts, dtype=jnp.int32))
    k_s, v_s, qsb = _qkv_rope(xs, w_qkv, cos_s, sin_s, n_heads, ts)
    att_s = _attn_sample(qsb, k_s, v_s, lam_p, sg, cache_k, cache_v, page_table, n_heads)
    xs = _proj_ln(att_s, xs, w_o, lng[1][1], lnb[1][1])
    y_sample = ffn_sub(xs, 1, 1).reshape(bs, ts, d)

    hw = 2 * HEAD_DIM
    return (y_prompt, y_sample, h_p, ca_p, cb_p,
            k_p.reshape(bp, tp, n_heads, hw), v_p.reshape(bp, tp, n_heads, hw),
            h_s, ca_s, cb_s,
            k_s.reshape(bs, ts, n_heads, hw), v_s.reshape(bs, ts, n_heads, hw))
```

```python
import functools
import math

import jax
import jax.numpy as jnp
from jax import lax
from jax.experimental import pallas as pl
from jax.experimental.pallas import tpu as pltpu

F32 = jnp.float32
BF16 = jnp.bfloat16

DEPTH = 2
LRU_BLOCKS = 8
LRU_C = 8.0
HEAD_DIM = 64
ROPE_THETA = 10000.0
ATT_SCALE = HEAD_DIM ** -0.5
LOG2_E = math.log2(math.e)
NEG_INF = -1e30
ALPHA = (2 * DEPTH) ** 0.25
LN_EPS = 1e-5
LAMBDA_INIT = 0.8 - 0.6 * math.exp(-0.3 * 1)

LANES = 128
SUBLANES = 8
VMEM_LIMIT_BYTES = 56 * 1024 * 1024

FFN_CHUNK = 256
FFN_ROWS = 512
FFN_STREAM_MAX_ROWS = 64
MIX_ROWS = 256
QKV_ROWS = 512
ATT_TILE = 512
ATT_HEADS = 4
ATT_QCHUNK = 512
ATT_LOOKAHEAD = 1
PROJ_ROWS = 512
DEC_PAGES = 8


def _layer_norm(y, g, b):
    mu = jnp.mean(y, axis=-1, keepdims=True)
    d = y - mu
    var = jnp.mean(d * d, axis=-1, keepdims=True)
    return d * lax.rsqrt(var + LN_EPS) * g + b


def _sigmoid(x):
    return 1.0 / (1.0 + jnp.exp(-x))


def _dot(a, b):
    return jnp.dot(a, b, preferred_element_type=F32)


def _dot_nt(a, b):
    return lax.dot_general(a, b, (((1,), (1,)), ((), ())), preferred_element_type=F32)


def _const_spec(shape):
    nd = len(shape)
    return pl.BlockSpec(shape, lambda *_: (0,) * nd)


def _resident_spec(shape):
    nd = len(shape)
    return pl.BlockSpec(shape, lambda *_: (0,) * nd, pipeline_mode=pl.Buffered(1))


def _params(*sem):
    return pltpu.CompilerParams(dimension_semantics=sem, vmem_limit_bytes=VMEM_LIMIT_BYTES)


def _ffn_kernel(*refs, n_chunks, with_proj):
    if with_proj:
        a_ref, xr_ref, wp_ref, gp_ref, bp_ref, wg_ref, wu_ref, wd_ref, g_ref, b_ref, o_ref = refs
        x = _layer_norm(ALPHA * xr_ref[...] + _dot(a_ref[...], wp_ref[...]), gp_ref[...], bp_ref[...])
    else:
        x_ref, wg_ref, wu_ref, wd_ref, g_ref, b_ref, o_ref = refs
        x = x_ref[...]
    xb = x.astype(BF16)
    acc = None
    for c in range(n_chunks):
        g = _dot(xb, wg_ref[c])
        u = _dot(xb, wu_ref[c])
        a = (g * _sigmoid(g) * u).astype(BF16)
        d = _dot(a, wd_ref[c])
        acc = d if acc is None else acc + d
    o_ref[...] = _layer_norm(ALPHA * x + 0.5 * acc, g_ref[...], b_ref[...])


def _ffn_stream_kernel(x_ref, wg_ref, wu_ref, wd_ref, g_ref, b_ref, o_ref, acc_ref):
    c = pl.program_id(0)
    x = x_ref[...]
    xb = x.astype(BF16)
    g = _dot(xb, wg_ref[...])
    u = _dot(xb, wu_ref[...])
    d = _dot((g * _sigmoid(g) * u).astype(BF16), wd_ref[...])

    @pl.when(c == 0)
    def _():
        acc_ref[...] = d

    @pl.when(c > 0)
    def _():
        acc_ref[...] += d

    @pl.when(c == pl.num_programs(0) - 1)
    def _():
        o_ref[...] = _layer_norm(ALPHA * x + 0.5 * acc_ref[...], g_ref[...], b_ref[...])


def _ffn_sublayer(x, wg, wu, wd, g, b, proj=None):
    m, d = x.shape
    n_chunks, _, fc = wg.shape
    if m <= FFN_STREAM_MAX_ROWS:
        if proj is not None:
            x = _proj_ln(proj[0], x, *proj[1:])
        return pl.pallas_call(
            _ffn_stream_kernel,
            grid=(n_chunks,),
            in_specs=[
                _const_spec((m, d)),
                pl.BlockSpec((None, d, fc), lambda c: (c, 0, 0)),
                pl.BlockSpec((None, d, fc), lambda c: (c, 0, 0)),
                pl.BlockSpec((None, fc, d), lambda c: (c, 0, 0)),
                _const_spec((1, d)),
                _const_spec((1, d)),
            ],
            out_specs=_const_spec((m, d)),
            out_shape=jax.ShapeDtypeStruct((m, d), F32),
            scratch_shapes=[pltpu.VMEM((m, d), F32)],
            compiler_params=_params("arbitrary"),
            name="ffn_stream",
        )(x, wg, wu, wd, g, b)
    tm = min(FFN_ROWS, m)
    rows = pl.BlockSpec((tm, d), lambda i: (i, 0))
    if proj is None:
        lead_args, lead_specs = (x,), [rows]
    else:
        a, wp, gp, bp = proj
        lead_args = (a, x, wp, gp, bp)
        lead_specs = [pl.BlockSpec((tm, a.shape[1]), lambda i: (i, 0)), rows,
                      _resident_spec(wp.shape), _const_spec((1, d)), _const_spec((1, d))]
    return pl.pallas_call(
        functools.partial(_ffn_kernel, n_chunks=n_chunks, with_proj=proj is not None),
        grid=(pl.cdiv(m, tm),),
        in_specs=lead_specs + [
            _resident_spec((n_chunks, d, fc)),
            _resident_spec((n_chunks, d, fc)),
            _resident_spec((n_chunks, fc, d)),
            _const_spec((1, d)),
            _const_spec((1, d)),
        ],
        out_specs=rows,
        out_shape=jax.ShapeDtypeStruct((m, d), F32),
        compiler_params=_params("parallel"),
        name="ffn_sublayer",
    )(*lead_args, wg, wu, wd, g, b)


def _softplus(x):
    return jnp.maximum(x, 0.0) + jnp.log1p(jnp.exp(-jnp.abs(x)))


def _gelu_tanh(x):
    c = math.sqrt(2.0 / math.pi)
    return 0.5 * x * (1.0 + jnp.tanh(c * (x + 0.044715 * (x * x * x))))


def _lru_gates(xa_c, wa, ba, wx, bx, lam):
    xb = xa_c.astype(BF16)
    r = _sigmoid(_dot(xb, wa) + ba)
    i = _sigmoid(_dot(xb, wx) + bx)
    log_a = (-LRU_C) * r * _softplus(-lam)
    a = jnp.exp(log_a)
    th = jnp.tanh(log_a)
    u = jnp.sqrt(-2.0 * th / (1.0 - th)) * (i * xa_c)
    return a, u


def _conv_b_post(acc, g, b):
    y = _layer_norm(acc, g, b)
    return y * _sigmoid(y)


HIST_A = 8
HIST_B = 32


def _mixer_prompt_kernel(x_ref, h0_ref, ca0_ref, cb0_ref, w_in_ref, caw_ref, cab_ref,
                         wa_ref, ba_ref, wx_ref, bx_ref, lam_ref, cbw_ref, cbb_ref,
                         clg_ref, clb_ref, wout_ref, g_ref, b_ref,
                         y_ref, hl_ref, cas_ref, cbs_ref,
                         ha_ref, hb_ref, sh_ref, sa_ref, su_ref, hc_ref, *, tm, pad, ka, kb):
    d_a = ha_ref.shape[1]
    t = pl.program_id(1)

    @pl.when(t == 0)
    def _():
        ha_ref[HIST_A - (ka - 1):HIST_A, :] = ca0_ref[...]
        hb_ref[HIST_B - (kb - 1):HIST_B, :] = cb0_ref[...]
        hc_ref[...] = h0_ref[...]
        sa_ref[0:pad, :] = jnp.ones((pad, d_a), F32)
        su_ref[0:pad, :] = jnp.zeros((pad, d_a), F32)

    x = x_ref[...]
    z = _dot(x.astype(BF16), w_in_ref[...])
    gate_a = z[:, 0:d_a]
    xa = z[:, d_a:2 * d_a]
    vb = z[:, 2 * d_a:3 * d_a]
    gb = z[:, 3 * d_a:4 * d_a]

    ha_ref[HIST_A:HIST_A + tm, :] = xa
    xa_c = cab_ref[...]
    for j in range(ka):
        s = HIST_A - (ka - 1) + j
        xa_c = xa_c + caw_ref[j:j + 1, :] * ha_ref[s:s + tm, :]
    a, u = _lru_gates(xa_c, wa_ref[...], ba_ref[...], wx_ref[...], bx_ref[...], lam_ref[...])
    sa_ref[pad:pad + tm, :] = a
    su_ref[pad:pad + tm, :] = u
    su_ref[pad:pad + 1, :] = u[0:1, :] + a[0:1, :] * hc_ref[...]
    k = 1
    while k < tm:
        a_cur = sa_ref[pad:pad + tm, :]
        u_cur = su_ref[pad:pad + tm, :]
        a_sh = sa_ref[pad - k:pad - k + tm, :]
        u_sh = su_ref[pad - k:pad - k + tm, :]
        su_ref[pad:pad + tm, :] = u_cur + a_cur * u_sh
        if 2 * k < tm:
            sa_ref[pad:pad + tm, :] = a_cur * a_sh
        k *= 2
    h_seq = su_ref[pad:pad + tm, :]
    h_last = h_seq[tm - 1:tm, :]
    hc_ref[...] = h_last
    hl_ref[...] = h_last
    ya = h_seq * _gelu_tanh(gate_a)

    hb_ref[HIST_B:HIST_B + tm, :] = vb * _sigmoid(gb)
    span = HIST_B - SUBLANES + tm
    acc = cbb_ref[...]
    for r in range(SUBLANES):
        taps = [j for j in range(kb) if (HIST_B - (kb - 1) + j) % SUBLANES == r]
        if not taps:
            continue
        if r:
            sh_ref[r - 1] = hb_ref[r:r + span, :]
        for j in taps:
            s = HIST_B - (kb - 1) + j - r
            rows = sh_ref[r - 1, s:s + tm, :] if r else hb_ref[s:s + tm, :]
            acc = acc + cbw_ref[j:j + 1, :] * rows
    yb = _conv_b_post(acc, clg_ref[...], clb_ref[...])

    m = _dot(ya.astype(BF16), wout_ref[0:d_a, :]) + _dot(yb.astype(BF16), wout_ref[d_a:2 * d_a, :])
    y_ref[...] = _layer_norm(ALPHA * x + m, g_ref[...], b_ref[...])

    tail_a = ha_ref[tm + HIST_A - (ka - 1):tm + HIST_A, :]
    cas_ref[...] = tail_a
    ha_ref[HIST_A - (ka - 1):HIST_A, :] = tail_a
    tail_b = hb_ref[tm + HIST_B - (kb - 1):tm + HIST_B, :]
    cbs_ref[...] = tail_b
    hb_ref[HIST_B - (kb - 1):HIST_B, :] = tail_b


def _mixer_prompt(x, h0, ca0, cb0, mp, g, b):
    bsz, t, d = x.shape
    d_a = h0.shape[-1]
    ka = ca0.shape[1] + 1
    kb = cb0.shape[1] + 1
    assert ka - 1 <= HIST_A and kb - 1 <= HIST_B
    tm = min(MIX_ROWS, t)
    assert t % tm == 0
    pad = max(SUBLANES, tm // 2)
    kernel = functools.partial(_mixer_prompt_kernel, tm=tm, pad=pad, ka=ka, kb=kb)
    bt = lambda bi, ti: (bi, ti, 0)
    b0 = lambda bi, ti: (bi, 0, 0)
    outs = pl.pallas_call(
        kernel,
        grid=(bsz, t // tm),
        in_specs=[
            pl.BlockSpec((None, tm, d), bt),
            pl.BlockSpec((None, 1, d_a), b0),
            pl.BlockSpec((None, ka - 1, d_a), b0),
            pl.BlockSpec((None, kb - 1, d_a), b0),
            _const_spec(mp["w_in"].shape),
            _const_spec((ka, d_a)), _const_spec((1, d_a)),
            _const_spec((d_a, d_a)), _const_spec((1, d_a)),
            _const_spec((d_a, d_a)), _const_spec((1, d_a)),
            _const_spec((1, d_a)),
            _const_spec((kb, d_a)), _const_spec((1, d_a)),
            _const_spec((1, d_a)), _const_spec((1, d_a)),
            _const_spec(mp["w_out"].shape),
            _const_spec((1, d)), _const_spec((1, d)),
        ],
        out_specs=[
            pl.BlockSpec((None, tm, d), bt),
            pl.BlockSpec((None, 1, d_a), b0),
            pl.BlockSpec((None, ka - 1, d_a), b0),
            pl.BlockSpec((None, kb - 1, d_a), b0),
        ],
        out_shape=[
            jax.ShapeDtypeStruct((bsz, t, d), F32),
            jax.ShapeDtypeStruct((bsz, 1, d_a), F32),
            jax.ShapeDtypeStruct((bsz, ka - 1, d_a), F32),
            jax.ShapeDtypeStruct((bsz, kb - 1, d_a), F32),
        ],
        scratch_shapes=[
            pltpu.VMEM((HIST_A + tm, d_a), F32),
            pltpu.VMEM((HIST_B + tm, d_a), F32),
            pltpu.VMEM((SUBLANES - 1, HIST_B - SUBLANES + tm, d_a), F32),
            pltpu.VMEM((pad + tm, d_a), F32),
            pltpu.VMEM((pad + tm, d_a), F32),
            pltpu.VMEM((1, d_a), F32),
        ],
        compiler_params=_params("parallel", "arbitrary"),
        name="mixer_prompt",
    )(x, h0.reshape(bsz, 1, d_a), ca0, cb0, mp["w_in"], mp["conv_a_w"], mp["conv_a_b"],
      mp["wa"], mp["ba"], mp["wx"], mp["bx"], mp["lam"], mp["conv_b_w"], mp["conv_b_b"],
      mp["cln_g"], mp["cln_b"], mp["w_out"], g, b)
    y, hl, cas, cbs = outs
    return y, hl.reshape(bsz, d_a), cas, cbs


def _mixer_sample_kernel(x_ref, h0_ref, ca_ref, cb_ref, w_in_ref, caw_ref, cab_ref,
                         wa_ref, ba_ref, wx_ref, bx_ref, lam_ref, cbw_ref, cbb_ref,
                         clg_ref, clb_ref, wout_ref, g_ref, b_ref,
                         y_ref, hl_ref, cas_ref, cbs_ref, *, ka, kb):
    d_a = h0_ref.shape[1]
    x = x_ref[...]
    z = _dot(x.astype(BF16), w_in_ref[...])
    gate_a = z[:, 0:d_a]
    xa = z[:, d_a:2 * d_a]
    vb = z[:, 2 * d_a:3 * d_a]
    gb = z[:, 3 * d_a:4 * d_a]

    xa_c = cab_ref[...]
    for j in range(ka - 1):
        xa_c = xa_c + caw_ref[j:j + 1, :] * ca_ref[j]
    xa_c = xa_c + caw_ref[ka - 1:ka, :] * xa
    a, u = _lru_gates(xa_c, wa_ref[...], ba_ref[...], wx_ref[...], bx_ref[...], lam_ref[...])
    h = a * h0_ref[...] + u
    hl_ref[...] = h
    ya = h * _gelu_tanh(gate_a)

    ub = vb * _sigmoid(gb)
    acc = cbb_ref[...]
    for j in range(kb - 1):
        acc = acc + cbw_ref[j:j + 1, :] * cb_ref[j]
    acc = acc + cbw_ref[kb - 1:kb, :] * ub
    yb = _conv_b_post(acc, clg_ref[...], clb_ref[...])

    m = _dot(ya.astype(BF16), wout_ref[0:d_a, :]) + _dot(yb.astype(BF16), wout_ref[d_a:2 * d_a, :])
    y_ref[...] = _layer_norm(ALPHA * x + m, g_ref[...], b_ref[...])

    for j in range(ka - 2):
        cas_ref[j] = ca_ref[j + 1]
    cas_ref[ka - 2] = xa
    for j in range(kb - 2):
        cbs_ref[j] = cb_ref[j + 1]
    cbs_ref[kb - 2] = ub


def _mixer_sample(x, h0, ca0, cb0, mp, g, b):
    n, d = x.shape
    d_a = h0.shape[-1]
    ka = ca0.shape[1] + 1
    kb = cb0.shape[1] + 1
    ca_t = jnp.swapaxes(ca0, 0, 1)
    cb_t = jnp.swapaxes(cb0, 0, 1)
    args = (x, h0, ca_t, cb_t, mp["w_in"], mp["conv_a_w"], mp["conv_a_b"],
            mp["wa"], mp["ba"], mp["wx"], mp["bx"], mp["lam"], mp["conv_b_w"], mp["conv_b_b"],
            mp["cln_g"], mp["cln_b"], mp["w_out"], g, b)
    y, hl, cas, cbs = pl.pallas_call(
        functools.partial(_mixer_sample_kernel, ka=ka, kb=kb),
        grid=(1,),
        in_specs=[_const_spec(a.shape) for a in args],
        out_specs=[_const_spec((n, d)), _const_spec((n, d_a)),
                   _const_spec((ka - 1, n, d_a)), _const_spec((kb - 1, n, d_a))],
        out_shape=[
            jax.ShapeDtypeStruct((n, d), F32),
            jax.ShapeDtypeStruct((n, d_a), F32),
            jax.ShapeDtypeStruct((ka - 1, n, d_a), F32),
            jax.ShapeDtypeStruct((kb - 1, n, d_a), F32),
        ],
        compiler_params=_params("arbitrary"),
        name="mixer_sample",
    )(*args)
    return y, hl, jnp.swapaxes(cas, 0, 1), jnp.swapaxes(cbs, 0, 1)


def _rope_tables(pos):
    half = HEAD_DIM // 2
    inv = ROPE_THETA ** (-jnp.arange(half, dtype=F32) / half)
    ang = pos.astype(F32)[:, None] * inv[None, :]
    cos = jnp.cos(ang)
    sin = jnp.sin(ang)
    c = jnp.concatenate([cos, cos, cos, cos], axis=-1)
    s = jnp.concatenate([-sin, sin, -sin, sin], axis=-1)
    return c, s


def _qkv_rope_kernel(x_ref, w_ref, cos_ref, sin_ref, k_ref, v_ref, qb_ref, *tile_refs,
                     n_heads, att_tile):
    aw = n_heads * 2 * HEAD_DIM
    tm = x_ref.shape[0]
    z = _dot(x_ref[...].astype(BF16), w_ref[...])
    c = cos_ref[...]
    s = sin_ref[...]
    hw = 2 * HEAD_DIM
    lane = lax.broadcasted_iota(jnp.int32, (1, hw), 1)
    first_half = (lane % HEAD_DIM) < (HEAD_DIM // 2)

    def rot(xh):
        swapped = jnp.where(first_half, pltpu.roll(xh, hw - HEAD_DIM // 2, axis=1),
                            pltpu.roll(xh, HEAD_DIM // 2, axis=1))
        return xh * c + swapped * s

    for h in range(n_heads):
        sl = slice(h * hw, (h + 1) * hw)
        qh = rot(z[:, h * hw:(h + 1) * hw])
        kh = rot(z[:, aw + h * hw:aw + (h + 1) * hw])
        vh = z[:, 2 * aw + h * hw:2 * aw + (h + 1) * hw]
        qb_ref[:, sl] = (qh * (ATT_SCALE * LOG2_E)).astype(BF16)
        k_ref[pl.ds(h, tm, stride=n_heads), :] = kh
        v_ref[pl.ds(h, tm, stride=n_heads), :] = vh
        if att_tile is not None:
            kb_ref, vt_ref = tile_refs
            kb_ref[:, sl] = kh.astype(BF16)
            for ti in range(vt_ref.shape[0]):
                vt_ref[ti, h] = vh[ti * att_tile:(ti + 1) * att_tile, :].T.astype(BF16)


def _qkv_rope(x, w_qkv, cos, sin, n_heads, t_per_seq, att_tile=None):
    m, d = x.shape
    aw = n_heads * 2 * HEAD_DIM
    hw = 2 * HEAD_DIM
    if t_per_seq == 1:
        tm = m
        tab_spec = pl.BlockSpec((1, hw), lambda i: (0, 0))
    else:
        tm = min(QKV_ROWS, t_per_seq)
        assert t_per_seq % tm == 0
        nt = t_per_seq // tm
        tab_spec = pl.BlockSpec((tm, hw), lambda i: (i % nt, 0))
    row = pl.BlockSpec((tm, aw), lambda i: (i, 0))
    cache_rows = pl.BlockSpec((tm * n_heads, hw), lambda i: (i, 0))
    out_specs = [cache_rows, cache_rows, row]
    out_shape = [jax.ShapeDtypeStruct((m * n_heads, hw), F32),
                 jax.ShapeDtypeStruct((m * n_heads, hw), F32),
                 jax.ShapeDtypeStruct((m, aw), BF16)]
    if att_tile is not None:
        assert tm % att_tile == 0
        tiles = tm // att_tile
        out_specs += [row, pl.BlockSpec((tiles, n_heads, hw, att_tile), lambda i: (i, 0, 0, 0))]
        out_shape += [jax.ShapeDtypeStruct((m, aw), BF16),
                      jax.ShapeDtypeStruct((m // att_tile, n_heads, hw, att_tile), BF16)]
    return pl.pallas_call(
        functools.partial(_qkv_rope_kernel, n_heads=n_heads, att_tile=att_tile),
        grid=(m // tm,),
        in_specs=[pl.BlockSpec((tm, d), lambda i: (i, 0)), _const_spec(w_qkv.shape),
                  tab_spec, tab_spec],
        out_specs=out_specs,
        out_shape=out_shape,
        compiler_params=_params("parallel"),
        name="qkv_rope",
    )(x, w_qkv, cos, sin)


def _diff_lambda(lq1, lk1, lq2, lk2):
    s1 = jnp.sum(lq1 * lk1, axis=-1, keepdims=True)
    s2 = jnp.sum(lq2 * lk2, axis=-1, keepdims=True)
    return jnp.exp(s1) - jnp.exp(s2) + LAMBDA_INIT


def _sub_norm(o, g):
    return o * lax.rsqrt(jnp.mean(o * o, axis=-1, keepdims=True) + LN_EPS) * g * (1.0 - LAMBDA_INIT)


def _attn_prompt_kernel(q_ref, k_ref, vt_ref, lq1_ref, lk1_ref, lq2_ref, lk2_ref, sg_ref, o_ref,
                        q2_ref, s_ref, m_ref, l_ref, acc_ref, *, t, heads, qc):
    i = pl.program_id(2)
    hw = 2 * HEAD_DIM
    lane = lax.broadcasted_iota(jnp.int32, (t, hw), 1)
    for g in range(heads):
        q = q_ref[:, g * hw:(g + 1) * hw]
        zero = jnp.zeros_like(q)
        q2_ref[g, 0:t, :] = jnp.where(lane < HEAD_DIM, q, zero)
        q2_ref[g, t:2 * t, :] = jnp.where(lane >= HEAD_DIM, q, zero)
        m_ref[g] = jnp.full((1, 2 * t), NEG_INF, F32)
        l_ref[g] = jnp.zeros((1, 2 * t), F32)
        acc_ref[g] = jnp.zeros((hw, 2 * t), F32)

    chains = [(g, c) for g in range(heads) for c in range(2 * t // qc)]
    ahead = min(ATT_LOOKAHEAD, len(chains))

    def scores(off, g, c):
        kt = k_ref[pl.ds(off, t), g * hw:(g + 1) * hw]
        return _dot_nt(kt, q2_ref[g, c * qc:(c + 1) * qc, :])

    def step(j, masked):
        off = pl.multiple_of(j * t, t)
        for idx, (g, c) in enumerate(chains):
            nxt = idx + ahead
            if nxt < len(chains):
                s_ref[nxt] = scores(off, *chains[nxt])
            elif not masked:
                s_ref[nxt - len(chains)] = scores(pl.multiple_of(off + t, t), *chains[nxt - len(chains)])
            cs = slice(c * qc, (c + 1) * qc)
            s = s_ref[idx]
            if masked:
                key_pos = lax.broadcasted_iota(jnp.int32, (t, qc), 0)
                qry_pos = lax.broadcasted_iota(jnp.int32, (t, qc), 1) + (c * qc) % t
                s = jnp.where(key_pos <= qry_pos, s, NEG_INF)
            m_old = m_ref[g, :, cs]
            m_new = jnp.maximum(m_old, jnp.max(s, axis=0, keepdims=True))
            alpha = jnp.exp2(m_old - m_new)
            p = jnp.exp2(s - m_new)
            l_ref[g, :, cs] = alpha * l_ref[g, :, cs] + jnp.sum(p, axis=0, keepdims=True)
            m_ref[g, :, cs] = m_new
            acc_ref[g, :, cs] = alpha * acc_ref[g, :, cs] + _dot(vt_ref[j, g], p.astype(BF16))

    def body(j, carry):
        step(j, False)
        return carry

    for idx in range(ahead):
        s_ref[idx] = scores(0, *chains[idx])
    lax.fori_loop(0, i, body, 0)
    step(i, True)

    lam = _diff_lambda(lq1_ref[...], lk1_ref[...], lq2_ref[...], lk2_ref[...])
    for g in range(heads):
        a = acc_ref[g] * (1.0 / l_ref[g])
        o = (a[:, 0:t] - lam * a[:, t:2 * t]).T
        o_ref[:, g * hw:(g + 1) * hw] = _sub_norm(o, sg_ref[...]).astype(o_ref.dtype)


def _attn_prompt(qb, kb, vt, lam_p, subln_g, bsz, t_seq, n_heads, t):
    m, aw = qb.shape
    hw = aw // n_heads
    heads = min(ATT_HEADS, n_heads)
    assert t_seq % t == 0 and n_heads % heads == 0
    nq = t_seq // t
    vec = _const_spec((1, HEAD_DIM))
    qc = math.gcd(ATT_QCHUNK, t)
    return pl.pallas_call(
        functools.partial(_attn_prompt_kernel, t=t, heads=heads, qc=qc),
        grid=(bsz, n_heads // heads, nq),
        in_specs=[
            pl.BlockSpec((t, heads * hw), lambda b, h, i: (b * nq + i, h)),
            pl.BlockSpec((t_seq, heads * hw), lambda b, h, i: (b, h)),
            pl.BlockSpec((nq, heads, hw, t), lambda b, h, i: (b, h, 0, 0)),
            vec, vec, vec, vec, _const_spec((1, hw)),
        ],
        out_specs=pl.BlockSpec((t, heads * hw), lambda b, h, i: (b * nq + i, h)),
        out_shape=jax.ShapeDtypeStruct((m, aw), BF16),
        scratch_shapes=[
            pltpu.VMEM((heads, 2 * t, hw), BF16),
            pltpu.VMEM((heads * (2 * t // qc), t, qc), F32),
            pltpu.VMEM((heads, 1, 2 * t), F32),
            pltpu.VMEM((heads, 1, 2 * t), F32),
            pltpu.VMEM((heads, hw, 2 * t), F32),
        ],
        compiler_params=_params("parallel", "parallel", "arbitrary"),
        name="attn_prompt",
    )(qb, kb, vt, *lam_p, subln_g)


def _attn_sample_kernel(pt_ref, q_ref, kn_ref, vn_ref, lq1_ref, lk1_ref, lq2_ref, lk2_ref, sg_ref,
                        *refs, n_pages, n_heads):
    k_refs = refs[:n_pages]
    v_refs = refs[n_pages:2 * n_pages]
    o_ref = refs[2 * n_pages]
    qm_ref, new_ref, s_ref, m_ref, l_ref, acc_ref = refs[2 * n_pages + 1:]
    p_idx = pl.program_id(1)
    hw = 2 * HEAD_DIM
    rows = 2 * n_heads
    page_rows = k_refs[0].shape[0]
    lane = lax.broadcasted_iota(jnp.int32, (1, hw), 1)

    @pl.when(p_idx == 0)
    def _():
        q = q_ref[...].astype(F32)
        for h in range(n_heads):
            qh = q[:, h * hw:(h + 1) * hw]
            qm_ref[h:h + 1, :] = jnp.where(lane < HEAD_DIM, qh, 0.0)
            qm_ref[n_heads + h:n_heads + h + 1, :] = jnp.where(lane >= HEAD_DIM, qh, 0.0)
        m_ref[...] = jnp.full(m_ref.shape, NEG_INF, F32)
        l_ref[...] = jnp.zeros(l_ref.shape, F32)
        acc_ref[...] = jnp.zeros(acc_ref.shape, F32)

    qm = qm_ref[...].astype(BF16)
    own = (lax.broadcasted_iota(jnp.int32, (rows, page_rows), 0) % n_heads
           == lax.broadcasted_iota(jnp.int32, (rows, page_rows), 1) % n_heads)
    for pg in range(n_pages):
        s_pg = _dot_nt(qm, k_refs[pg][...].astype(BF16))
        s_ref[:, pg * page_rows:(pg + 1) * page_rows] = jnp.where(own, s_pg, NEG_INF)
    s = s_ref[...]
    m_old = m_ref[...]
    m_new = jnp.maximum(m_old, jnp.max(s, axis=-1, keepdims=True))
    alpha = jnp.exp2(m_old - m_new)
    p = jnp.exp2(s - m_new)
    l_ref[...] = alpha * l_ref[...] + jnp.sum(p, axis=-1, keepdims=True)
    m_ref[...] = m_new
    pv = None
    for pg in range(n_pages):
        d = _dot(p[:, pg * page_rows:(pg + 1) * page_rows].astype(BF16), v_refs[pg][...].astype(BF16))
        pv = d if pv is None else pv + d
    acc_ref[...] = alpha * acc_ref[...] + pv

    @pl.when(p_idx == pl.num_programs(1) - 1)
    def _():
        lam = _diff_lambda(lq1_ref[...], lk1_ref[...], lq2_ref[...], lk2_ref[...])
        kn = kn_ref[...].astype(BF16).astype(F32)
        vn = vn_ref[...].astype(BF16).astype(F32)
        for c in range(2):
            new_ref[0, c * n_heads:(c + 1) * n_heads, :] = kn
            new_ref[1, c * n_heads:(c + 1) * n_heads, :] = vn
        s_new = jnp.sum(qm_ref[...] * new_ref[0], axis=-1, keepdims=True)
        m_old = m_ref[...]
        m_new = jnp.maximum(m_old, s_new)
        alpha = jnp.exp2(m_old - m_new)
        p_new = jnp.exp2(s_new - m_new)
        l = alpha * l_ref[...] + p_new
        acc = alpha * acc_ref[...] + p_new.astype(BF16).astype(F32) * new_ref[1]
        o = acc / l
        o = _sub_norm(o[0:n_heads, :] - lam * o[n_heads:rows, :], sg_ref[...])
        for h in range(n_heads):
            o_ref[:, h * hw:(h + 1) * hw] = o[h:h + 1, :].astype(o_ref.dtype)


def _attn_sample(qb, k_new, v_new, lam_p, subln_g, cache_k, cache_v, page_table, n_heads):
    n, aw = qb.shape
    hw = aw // n_heads
    page = cache_k.shape[1]
    n_log = page_table.shape[1]
    pp = DEC_PAGES
    while n_log % pp:
        pp //= 2
    row3 = lambda a: a.reshape(n, 1, aw)
    row_spec = pl.BlockSpec((None, 1, aw), lambda b, p, pt: (b, 0, 0))
    new_spec = pl.BlockSpec((None, n_heads, hw), lambda b, p, pt: (b, 0, 0))
    vec = pl.BlockSpec((1, HEAD_DIM), lambda b, p, pt: (0, 0))

    def page_spec(j):
        return pl.BlockSpec((None, page * n_heads, hw), lambda b, p, pt: (pt[b, p * pp + j], 0, 0))

    grid_spec = pltpu.PrefetchScalarGridSpec(
        num_scalar_prefetch=1,
        grid=(n, n_log // pp),
        in_specs=[row_spec, new_spec, new_spec, vec, vec, vec, vec,
                  pl.BlockSpec((1, hw), lambda b, p, pt: (0, 0))]
                 + [page_spec(j) for j in range(pp)] + [page_spec(j) for j in range(pp)],
        out_specs=row_spec,
        scratch_shapes=[
            pltpu.VMEM((2 * n_heads, hw), F32),
            pltpu.VMEM((2, 2 * n_heads, hw), F32),
            pltpu.VMEM((2 * n_heads, pp * page * n_heads), F32),
            pltpu.VMEM((2 * n_heads, 1), F32),
            pltpu.VMEM((2 * n_heads, 1), F32),
            pltpu.VMEM((2 * n_heads, hw), F32),
        ],
    )
    ck = cache_k.reshape(cache_k.shape[0], page * n_heads, hw)
    cv = cache_v.reshape(cache_v.shape[0], page * n_heads, hw)
    out = pl.pallas_call(
        functools.partial(_attn_sample_kernel, n_pages=pp, n_heads=n_heads),
        grid_spec=grid_spec,
        out_shape=jax.ShapeDtypeStruct((n, 1, aw), BF16),
        compiler_params=_params("parallel", "arbitrary"),
        name="attn_sample",
    )(page_table, row3(qb), k_new.reshape(n, n_heads, hw), v_new.reshape(n, n_heads, hw),
      *lam_p, subln_g, *([ck] * pp), *([cv] * pp))
    return out.reshape(n, aw)


def _proj_ln_kernel(a_ref, x_ref, w_ref, g_ref, b_ref, o_ref):
    y = _dot(a_ref[...], w_ref[...])
    o_ref[...] = _layer_norm(ALPHA * x_ref[...] + y, g_ref[...], b_ref[...])


def _proj_ln(a, x, w, g, b):
    m, d = x.shape
    tm = min(PROJ_ROWS, m)
    return pl.pallas_call(
        _proj_ln_kernel,
        grid=(pl.cdiv(m, tm),),
        in_specs=[pl.BlockSpec((tm, a.shape[1]), lambda i: (i, 0)),
                  pl.BlockSpec((tm, d), lambda i: (i, 0)),
                  _const_spec(w.shape), _const_spec((1, d)), _const_spec((1, d))],
        out_specs=pl.BlockSpec((tm, d), lambda i: (i, 0)),
        out_shape=jax.ShapeDtypeStruct((m, d), F32),
        compiler_params=_params("parallel"),
        name="proj_ln",
    )(a, x, w, g, b)


def _block_diag(w):
    n, bi, bj = w.shape
    eye = jnp.eye(n, dtype=w.dtype)
    return jnp.einsum("nij,nm->nimj", w, eye).reshape(n * bi, n * bj)


def _chunk_cols(w):
    d, f = w.shape
    return jnp.swapaxes(w.reshape(d, f // FFN_CHUNK, FFN_CHUNK), 0, 1).astype(BF16)


def _chunk_rows(w):
    f, d = w.shape
    return w.reshape(f // FFN_CHUNK, FFN_CHUNK, d).astype(BF16)


def kernel(x_prompt, x_sample, state_lru_h, state_conv_a, state_conv_b, cache_k, cache_v, page_table,
           ln_g, ln_b, ffn_w_gate, ffn_w_up, ffn_w_down,
           rec_w_in, conv_a_w, conv_a_b, lru_w_a, lru_b_a, lru_w_x, lru_b_x, lru_lambda,
           conv_b_w, conv_b_b, conv_ln_g, conv_ln_b, rec_w_out,
           att_w_qkv, lambda_q1, lambda_k1, lambda_q2, lambda_k2, subln_g, att_w_out):
    bp, tp, d = x_prompt.shape
    bs, ts, _ = x_sample.shape
    assert ts == 1, "the sample group decodes one token per sequence"
    n_heads = cache_k.shape[2]
    d_a = state_lru_h.shape[-1]
    d_ff = ffn_w_gate.shape[-1]
    assert d_ff % FFN_CHUNK == 0
    row = lambda v: v.reshape(1, -1)

    ffn = [[(_chunk_cols(ffn_w_gate[l, i]), _chunk_cols(ffn_w_up[l, i]), _chunk_rows(ffn_w_down[l, i]))
            for i in range(2)] for l in range(DEPTH)]
    lng = [[row(ln_g[l, i]) for i in range(3)] for l in range(DEPTH)]
    lnb = [[row(ln_b[l, i]) for i in range(3)] for l in range(DEPTH)]
    mp = dict(
        w_in=rec_w_in.astype(BF16), conv_a_w=conv_a_w, conv_a_b=row(conv_a_b),
        wa=_block_diag(lru_w_a).astype(BF16), ba=row(lru_b_a),
        wx=_block_diag(lru_w_x).astype(BF16), bx=row(lru_b_x), lam=row(lru_lambda),
        conv_b_w=conv_b_w, conv_b_b=row(conv_b_b), cln_g=row(conv_ln_g), cln_b=row(conv_ln_b),
        w_out=rec_w_out.astype(BF16))
    w_qkv = att_w_qkv.astype(BF16)
    w_o = att_w_out.astype(BF16)
    lam_p = (row(lambda_q1), row(lambda_k1), row(lambda_q2), row(lambda_k2))
    sg = row(subln_g)

    def ffn_sub(x, l, i, proj=None):
        return _ffn_sublayer(x, *ffn[l][i], lng[l][2 * i], lnb[l][2 * i], proj)

    x = x_prompt.reshape(bp * tp, d)
    x = ffn_sub(x, 0, 0)
    x, h_p, ca_p, cb_p = _mixer_prompt(
        x.reshape(bp, tp, d), jnp.zeros((bp, d_a), F32),
        jnp.zeros((bp, state_conv_a.shape[1], d_a), F32),
        jnp.zeros((bp, state_conv_b.shape[1], d_a), F32), mp, lng[0][1], lnb[0][1])
    x = ffn_sub(x.reshape(bp * tp, d), 0, 1)
    x = ffn_sub(x, 1, 0)
    cos_p, sin_p = _rope_tables(jnp.arange(tp, dtype=jnp.int32))
    att_tile = min(ATT_TILE, tp)
    k_p, v_p, qb, kb, vt = _qkv_rope(x, w_qkv, cos_p, sin_p, n_heads, tp, att_tile)
    att = _attn_prompt(qb, kb, vt, lam_p, sg, bp, tp, n_heads, att_tile)
    y_prompt = ffn_sub(x, 1, 1, (att, w_o, lng[1][1], lnb[1][1])).reshape(bp, tp, d)

    past = page_table.shape[1] * cache_k.shape[1]
    xs = x_sample.reshape(bs, d)
    xs = ffn_sub(xs, 0, 0)
    xs, h_s, ca_s, cb_s = _mixer_sample(xs, state_lru_h, state_conv_a, state_conv_b, mp,
                                        lng[0][1], lnb[0][1])
    xs = ffn_sub(xs, 0, 1)
    xs = ffn_sub(xs, 1, 0)
    cos_s, sin_s = _rope_tables(past + jnp.arange(ts, dtype=jnp.int32))
    k_s, v_s, qsb = _qkv_rope(xs, w_qkv, cos_s, sin_s, n_heads, ts)
    att_s = _attn_sample(qsb, k_s, v_s, lam_p, sg, cache_k, cache_v, page_table, n_heads)
    y_sample = ffn_sub(xs, 1, 1, (att_s, w_o, lng[1][1], lnb[1][1])).reshape(bs, ts, d)

    hw = 2 * HEAD_DIM
    return (y_prompt, y_sample, h_p, ca_p, cb_p,
            k_p.reshape(bp, tp, n_heads, hw), v_p.reshape(bp, tp, n_heads, hw),
            h_s, ca_s, cb_s,
            k_s.reshape(bs, ts, n_heads, hw), v_s.reshape(bs, ts, n_heads, hw))
```

```python
import functools
import math

import jax
import jax.numpy as jnp
from jax import lax
from jax.experimental import pallas as pl
from jax.experimental.pallas import tpu as pltpu

F32 = jnp.float32
BF16 = jnp.bfloat16

DEPTH = 2
LRU_BLOCKS = 8
LRU_C = 8.0
HEAD_DIM = 64
ROPE_THETA = 10000.0
ATT_SCALE = HEAD_DIM ** -0.5
LOG2_E = math.log2(math.e)
NEG_INF = -1e30
ALPHA = (2 * DEPTH) ** 0.25
LN_EPS = 1e-5
LAMBDA_INIT = 0.8 - 0.6 * math.exp(-0.3 * 1)

LANES = 128
SUBLANES = 8
VMEM_LIMIT_BYTES = 56 * 1024 * 1024

FFN_CHUNK = 256
FFN_ROWS = 512
FFN_STREAM_MAX_ROWS = 64
MIX_ROWS = 256
QKV_ROWS = 512
ATT_TILE = 512
ATT_HEADS = 4
ATT_QCHUNK = 512
ATT_LOOKAHEAD = 1
PROJ_ROWS = 512
DEC_PAGES = 8


def _layer_norm(y, g, b):
    mu = jnp.mean(y, axis=-1, keepdims=True)
    d = y - mu
    var = jnp.mean(d * d, axis=-1, keepdims=True)
    return d * lax.rsqrt(var + LN_EPS) * g + b


def _sigmoid(x):
    return 1.0 / (1.0 + jnp.exp(-x))


def _dot(a, b):
    return jnp.dot(a, b, preferred_element_type=F32)


def _dot_nt(a, b):
    return lax.dot_general(a, b, (((1,), (1,)), ((), ())), preferred_element_type=F32)


def _const_spec(shape):
    nd = len(shape)
    return pl.BlockSpec(shape, lambda *_: (0,) * nd)


def _resident_spec(shape):
    nd = len(shape)
    return pl.BlockSpec(shape, lambda *_: (0,) * nd, pipeline_mode=pl.Buffered(1))


def _params(*sem):
    return pltpu.CompilerParams(dimension_semantics=sem, vmem_limit_bytes=VMEM_LIMIT_BYTES)


def _swiglu(xb, wg_ref, wu_ref, wd_ref):
    d_ff = wg_ref.shape[1]
    acc = None
    for c in range(d_ff // FFN_CHUNK):
        cols = slice(c * FFN_CHUNK, (c + 1) * FFN_CHUNK)
        g = _dot(xb, wg_ref[:, cols])
        u = _dot(xb, wu_ref[:, cols])
        d = _dot((g * _sigmoid(g) * u).astype(BF16), wd_ref[cols, :])
        acc = d if acc is None else acc + d
    return acc


def _ffn_kernel(*refs, with_proj):
    if with_proj:
        a_ref, xr_ref, wp_ref, gp_ref, bp_ref, wg_ref, wu_ref, wd_ref, g_ref, b_ref, o_ref = refs
        x = _layer_norm(ALPHA * xr_ref[...] + _dot(a_ref[...], wp_ref[...]), gp_ref[...], bp_ref[...])
    else:
        x_ref, wg_ref, wu_ref, wd_ref, g_ref, b_ref, o_ref = refs
        x = x_ref[...]
    acc = _swiglu(x.astype(BF16), wg_ref, wu_ref, wd_ref)
    o_ref[...] = _layer_norm(ALPHA * x + 0.5 * acc, g_ref[...], b_ref[...])


def _ffn_stream_kernel(x_ref, wg_ref, wu_ref, wd_ref, g_ref, b_ref, o_ref, acc_ref):
    c = pl.program_id(0)
    x = x_ref[...]
    xb = x.astype(BF16)
    g = _dot(xb, wg_ref[...])
    u = _dot(xb, wu_ref[...])
    d = _dot((g * _sigmoid(g) * u).astype(BF16), wd_ref[...])

    @pl.when(c == 0)
    def _():
        acc_ref[...] = d

    @pl.when(c > 0)
    def _():
        acc_ref[...] += d

    @pl.when(c == pl.num_programs(0) - 1)
    def _():
        o_ref[...] = _layer_norm(ALPHA * x + 0.5 * acc_ref[...], g_ref[...], b_ref[...])


def _ffn_sublayer(x, wg, wu, wd, layer, idx, g, b, proj=None):
    m, d = x.shape
    d_ff = wg.shape[-1]
    fc = FFN_CHUNK
    assert d_ff % fc == 0
    if m <= FFN_STREAM_MAX_ROWS:
        if proj is not None:
            x = _proj_ln(proj[0], x, *proj[1:])
        return pl.pallas_call(
            _ffn_stream_kernel,
            grid=(d_ff // fc,),
            in_specs=[
                _const_spec((m, d)),
                pl.BlockSpec((None, None, d, fc), lambda c: (layer, idx, 0, c)),
                pl.BlockSpec((None, None, d, fc), lambda c: (layer, idx, 0, c)),
                pl.BlockSpec((None, None, fc, d), lambda c: (layer, idx, c, 0)),
                _const_spec((1, d)),
                _const_spec((1, d)),
            ],
            out_specs=_const_spec((m, d)),
            out_shape=jax.ShapeDtypeStruct((m, d), F32),
            scratch_shapes=[pltpu.VMEM((m, d), F32)],
            compiler_params=_params("arbitrary"),
            name="ffn_stream",
        )(x, wg, wu, wd, g, b)
    tm = min(FFN_ROWS, m)
    rows = pl.BlockSpec((tm, d), lambda i: (i, 0))
    if proj is None:
        lead_args, lead_specs = (x,), [rows]
    else:
        a, wp, gp, bp = proj
        lead_args = (a, x, wp, gp, bp)
        lead_specs = [pl.BlockSpec((tm, a.shape[1]), lambda i: (i, 0)), rows,
                      _resident_spec(wp.shape), _const_spec((1, d)), _const_spec((1, d))]
    pick = lambda *_: (layer, idx, 0, 0)
    resident = pl.Buffered(1)
    return pl.pallas_call(
        functools.partial(_ffn_kernel, with_proj=proj is not None),
        grid=(pl.cdiv(m, tm),),
        in_specs=lead_specs + [
            pl.BlockSpec((None, None, d, d_ff), pick, pipeline_mode=resident),
            pl.BlockSpec((None, None, d, d_ff), pick, pipeline_mode=resident),
            pl.BlockSpec((None, None, d_ff, d), pick, pipeline_mode=resident),
            _const_spec((1, d)),
            _const_spec((1, d)),
        ],
        out_specs=rows,
        out_shape=jax.ShapeDtypeStruct((m, d), F32),
        compiler_params=_params("parallel"),
        name="ffn_sublayer",
    )(*lead_args, wg, wu, wd, g, b)


def _softplus(x):
    return jnp.maximum(x, 0.0) + jnp.log1p(jnp.exp(-jnp.abs(x)))


def _gelu_tanh(x):
    c = math.sqrt(2.0 / math.pi)
    return 0.5 * x * (1.0 + jnp.tanh(c * (x + 0.044715 * (x * x * x))))


def _lru_gates(xa_c, wa, ba, wx, bx, lam):
    xb = xa_c.astype(BF16)
    r = _sigmoid(_dot(xb, wa) + ba)
    i = _sigmoid(_dot(xb, wx) + bx)
    log_a = (-LRU_C) * r * _softplus(-lam)
    a = jnp.exp(log_a)
    th = jnp.tanh(log_a)
    y = -2.0 * th / (1.0 - th)
    u = jnp.where(y > 0.0, y * lax.rsqrt(y), 0.0) * (i * xa_c)
    return a, u


def _conv_b_post(acc, g, b):
    y = _layer_norm(acc, g, b)
    return y * _sigmoid(y)


HIST_A = 8
HIST_B = 32


def _mixer_prompt_kernel(x_ref, h0_ref, ca0_ref, cb0_ref, w_in_ref, caw_ref, cab_ref,
                         wa_ref, ba_ref, wx_ref, bx_ref, lam_ref, cbw_ref, cbb_ref,
                         clg_ref, clb_ref, wout_ref, g_ref, b_ref,
                         y_ref, hl_ref, cas_ref, cbs_ref,
                         ha_ref, hb_ref, sh_ref, sa_ref, su_ref, hc_ref, *, tm, pad, ka, kb):
    d_a = ha_ref.shape[1]
    t = pl.program_id(1)

    @pl.when(t == 0)
    def _():
        ha_ref[HIST_A - (ka - 1):HIST_A, :] = ca0_ref[...]
        hb_ref[HIST_B - (kb - 1):HIST_B, :] = cb0_ref[...]
        hc_ref[...] = h0_ref[...]
        sa_ref[0:pad, :] = jnp.ones((pad, d_a), F32)
        su_ref[0:pad, :] = jnp.zeros((pad, d_a), F32)

    x = x_ref[...]
    z = _dot(x.astype(BF16), w_in_ref[...])
    gate_a = z[:, 0:d_a]
    xa = z[:, d_a:2 * d_a]
    vb = z[:, 2 * d_a:3 * d_a]
    gb = z[:, 3 * d_a:4 * d_a]

    ha_ref[HIST_A:HIST_A + tm, :] = xa
    xa_c = cab_ref[...]
    for j in range(ka):
        s = HIST_A - (ka - 1) + j
        xa_c = xa_c + caw_ref[j:j + 1, :] * ha_ref[s:s + tm, :]
    a, u = _lru_gates(xa_c, wa_ref[...], ba_ref[...], wx_ref[...], bx_ref[...], lam_ref[...])
    sa_ref[pad:pad + tm, :] = a
    su_ref[pad:pad + tm, :] = u
    su_ref[pad:pad + 1, :] = u[0:1, :] + a[0:1, :] * hc_ref[...]
    k = 1
    while k < tm:
        a_cur = sa_ref[pad:pad + tm, :]
        u_cur = su_ref[pad:pad + tm, :]
        a_sh = sa_ref[pad - k:pad - k + tm, :]
        u_sh = su_ref[pad - k:pad - k + tm, :]
        su_ref[pad:pad + tm, :] = u_cur + a_cur * u_sh
        if 2 * k < tm:
            sa_ref[pad:pad + tm, :] = a_cur * a_sh
        k *= 2
    h_seq = su_ref[pad:pad + tm, :]
    h_last = h_seq[tm - 1:tm, :]
    hc_ref[...] = h_last
    hl_ref[...] = h_last
    ya = h_seq * _gelu_tanh(gate_a)

    hb_ref[HIST_B:HIST_B + tm, :] = vb * _sigmoid(gb)
    span = HIST_B - SUBLANES + tm
    acc = cbb_ref[...]
    for r in range(SUBLANES):
        taps = [j for j in range(kb) if (HIST_B - (kb - 1) + j) % SUBLANES == r]
        if not taps:
            continue
        if r:
            sh_ref[r - 1] = hb_ref[r:r + span, :]
        for j in taps:
            s = HIST_B - (kb - 1) + j - r
            rows = sh_ref[r - 1, s:s + tm, :] if r else hb_ref[s:s + tm, :]
            acc = acc + cbw_ref[j:j + 1, :] * rows
    yb = _conv_b_post(acc, clg_ref[...], clb_ref[...])

    m = _dot(ya.astype(BF16), wout_ref[0:d_a, :]) + _dot(yb.astype(BF16), wout_ref[d_a:2 * d_a, :])
    y_ref[...] = _layer_norm(ALPHA * x + m, g_ref[...], b_ref[...])

    tail_a = ha_ref[tm + HIST_A - (ka - 1):tm + HIST_A, :]
    cas_ref[...] = tail_a
    ha_ref[HIST_A - (ka - 1):HIST_A, :] = tail_a
    tail_b = hb_ref[tm + HIST_B - (kb - 1):tm + HIST_B, :]
    cbs_ref[...] = tail_b
    hb_ref[HIST_B - (kb - 1):HIST_B, :] = tail_b


def _mixer_prompt(x, h0, ca0, cb0, mp, g, b):
    bsz, t, d = x.shape
    d_a = h0.shape[-1]
    ka = ca0.shape[1] + 1
    kb = cb0.shape[1] + 1
    assert ka - 1 <= HIST_A and kb - 1 <= HIST_B
    tm = min(MIX_ROWS, t)
    assert t % tm == 0
    pad = max(SUBLANES, tm // 2)
    kernel = functools.partial(_mixer_prompt_kernel, tm=tm, pad=pad, ka=ka, kb=kb)
    bt = lambda bi, ti: (bi, ti, 0)
    b0 = lambda bi, ti: (bi, 0, 0)
    outs = pl.pallas_call(
        kernel,
        grid=(bsz, t // tm),
        in_specs=[
            pl.BlockSpec((None, tm, d), bt),
            pl.BlockSpec((None, 1, d_a), b0),
            pl.BlockSpec((None, ka - 1, d_a), b0),
            pl.BlockSpec((None, kb - 1, d_a), b0),
            _const_spec(mp["w_in"].shape),
            _const_spec((ka, d_a)), _const_spec((1, d_a)),
            _const_spec((d_a, d_a)), _const_spec((1, d_a)),
            _const_spec((d_a, d_a)), _const_spec((1, d_a)),
            _const_spec((1, d_a)),
            _const_spec((kb, d_a)), _const_spec((1, d_a)),
            _const_spec((1, d_a)), _const_spec((1, d_a)),
            _const_spec(mp["w_out"].shape),
            _const_spec((1, d)), _const_spec((1, d)),
        ],
        out_specs=[
            pl.BlockSpec((None, tm, d), bt),
            pl.BlockSpec((None, 1, d_a), b0),
            pl.BlockSpec((None, ka - 1, d_a), b0),
            pl.BlockSpec((None, kb - 1, d_a), b0),
        ],
        out_shape=[
            jax.ShapeDtypeStruct((bsz, t, d), F32),
            jax.ShapeDtypeStruct((bsz, 1, d_a), F32),
            jax.ShapeDtypeStruct((bsz, ka - 1, d_a), F32),
            jax.ShapeDtypeStruct((bsz, kb - 1, d_a), F32),
        ],
        scratch_shapes=[
            pltpu.VMEM((HIST_A + tm, d_a), F32),
            pltpu.VMEM((HIST_B + tm, d_a), F32),
            pltpu.VMEM((SUBLANES - 1, HIST_B - SUBLANES + tm, d_a), F32),
            pltpu.VMEM((pad + tm, d_a), F32),
            pltpu.VMEM((pad + tm, d_a), F32),
            pltpu.VMEM((1, d_a), F32),
        ],
        compiler_params=_params("parallel", "arbitrary"),
        name="mixer_prompt",
    )(x, h0.reshape(bsz, 1, d_a), ca0, cb0, mp["w_in"], mp["conv_a_w"], mp["conv_a_b"],
      mp["wa"], mp["ba"], mp["wx"], mp["bx"], mp["lam"], mp["conv_b_w"], mp["conv_b_b"],
      mp["cln_g"], mp["cln_b"], mp["w_out"], g, b)
    y, hl, cas, cbs = outs
    return y, hl.reshape(bsz, d_a), cas, cbs


def _mixer_sample_kernel(x_ref, h0_ref, ca_ref, cb_ref, w_in_ref, caw_ref, cab_ref,
                         wa_ref, ba_ref, wx_ref, bx_ref, lam_ref, cbw_ref, cbb_ref,
                         clg_ref, clb_ref, wout_ref, g_ref, b_ref,
                         y_ref, hl_ref, cas_ref, cbs_ref, *, ka, kb):
    d_a = h0_ref.shape[1]
    x = x_ref[...]
    z = _dot(x.astype(BF16), w_in_ref[...])
    gate_a = z[:, 0:d_a]
    xa = z[:, d_a:2 * d_a]
    vb = z[:, 2 * d_a:3 * d_a]
    gb = z[:, 3 * d_a:4 * d_a]

    xa_c = cab_ref[...]
    for j in range(ka - 1):
        xa_c = xa_c + caw_ref[j:j + 1, :] * ca_ref[j]
    xa_c = xa_c + caw_ref[ka - 1:ka, :] * xa
    a, u = _lru_gates(xa_c, wa_ref[...], ba_ref[...], wx_ref[...], bx_ref[...], lam_ref[...])
    h = a * h0_ref[...] + u
    hl_ref[...] = h
    ya = h * _gelu_tanh(gate_a)

    ub = vb * _sigmoid(gb)
    acc = cbb_ref[...]
    for j in range(kb - 1):
        acc = acc + cbw_ref[j:j + 1, :] * cb_ref[j]
    acc = acc + cbw_ref[kb - 1:kb, :] * ub
    yb = _conv_b_post(acc, clg_ref[...], clb_ref[...])

    m = _dot(ya.astype(BF16), wout_ref[0:d_a, :]) + _dot(yb.astype(BF16), wout_ref[d_a:2 * d_a, :])
    y_ref[...] = _layer_norm(ALPHA * x + m, g_ref[...], b_ref[...])

    for j in range(ka - 2):
        cas_ref[j] = ca_ref[j + 1]
    cas_ref[ka - 2] = xa
    for j in range(kb - 2):
        cbs_ref[j] = cb_ref[j + 1]
    cbs_ref[kb - 2] = ub


def _mixer_sample(x, h0, ca0, cb0, mp, g, b):
    n, d = x.shape
    d_a = h0.shape[-1]
    ka = ca0.shape[1] + 1
    kb = cb0.shape[1] + 1
    ca_t = jnp.swapaxes(ca0, 0, 1)
    cb_t = jnp.swapaxes(cb0, 0, 1)
    args = (x, h0, ca_t, cb_t, mp["w_in"], mp["conv_a_w"], mp["conv_a_b"],
            mp["wa"], mp["ba"], mp["wx"], mp["bx"], mp["lam"], mp["conv_b_w"], mp["conv_b_b"],
            mp["cln_g"], mp["cln_b"], mp["w_out"], g, b)
    y, hl, cas, cbs = pl.pallas_call(
        functools.partial(_mixer_sample_kernel, ka=ka, kb=kb),
        grid=(1,),
        in_specs=[_const_spec(a.shape) for a in args],
        out_specs=[_const_spec((n, d)), _const_spec((n, d_a)),
                   _const_spec((ka - 1, n, d_a)), _const_spec((kb - 1, n, d_a))],
        out_shape=[
            jax.ShapeDtypeStruct((n, d), F32),
            jax.ShapeDtypeStruct((n, d_a), F32),
            jax.ShapeDtypeStruct((ka - 1, n, d_a), F32),
            jax.ShapeDtypeStruct((kb - 1, n, d_a), F32),
        ],
        compiler_params=_params("arbitrary"),
        name="mixer_sample",
    )(*args)
    return y, hl, jnp.swapaxes(cas, 0, 1), jnp.swapaxes(cbs, 0, 1)


def _rope_tables(pos):
    half = HEAD_DIM // 2
    inv = ROPE_THETA ** (-jnp.arange(half, dtype=F32) / half)
    ang = pos.astype(F32)[:, None] * inv[None, :]
    cos = jnp.cos(ang)
    sin = jnp.sin(ang)
    c = jnp.concatenate([cos, cos, cos, cos], axis=-1)
    s = jnp.concatenate([-sin, sin, -sin, sin], axis=-1)
    return c, s


def _qkv_rope_kernel(x_ref, w_ref, cos_ref, sin_ref, k_ref, v_ref, qb_ref, *tile_refs,
                     n_heads, att_tile):
    aw = n_heads * 2 * HEAD_DIM
    tm = x_ref.shape[0]
    z = _dot(x_ref[...].astype(BF16), w_ref[...])
    c = cos_ref[...]
    s = sin_ref[...]
    hw = 2 * HEAD_DIM
    lane = lax.broadcasted_iota(jnp.int32, (1, hw), 1)
    first_half = (lane % HEAD_DIM) < (HEAD_DIM // 2)

    def rot(xh):
        swapped = jnp.where(first_half, pltpu.roll(xh, hw - HEAD_DIM // 2, axis=1),
                            pltpu.roll(xh, HEAD_DIM // 2, axis=1))
        return xh * c + swapped * s

    for h in range(n_heads):
        sl = slice(h * hw, (h + 1) * hw)
        qh = rot(z[:, h * hw:(h + 1) * hw])
        kh = rot(z[:, aw + h * hw:aw + (h + 1) * hw])
        vh = z[:, 2 * aw + h * hw:2 * aw + (h + 1) * hw]
        qb_ref[:, sl] = (qh * (ATT_SCALE * LOG2_E)).astype(BF16)
        k_ref[pl.ds(h, tm, stride=n_heads), :] = kh
        v_ref[pl.ds(h, tm, stride=n_heads), :] = vh
        if att_tile is not None:
            kb_ref, vt_ref = tile_refs
            kb_ref[:, sl] = kh.astype(BF16)
            for ti in range(vt_ref.shape[0]):
                vt_ref[ti, h] = vh[ti * att_tile:(ti + 1) * att_tile, :].T.astype(BF16)


def _qkv_rope(x, w_qkv, cos, sin, n_heads, t_per_seq, att_tile=None):
    m, d = x.shape
    aw = n_heads * 2 * HEAD_DIM
    hw = 2 * HEAD_DIM
    if t_per_seq == 1:
        tm = m
        tab_spec = pl.BlockSpec((1, hw), lambda i: (0, 0))
    else:
        tm = min(QKV_ROWS, t_per_seq)
        assert t_per_seq % tm == 0
        nt = t_per_seq // tm
        tab_spec = pl.BlockSpec((tm, hw), lambda i: (i % nt, 0))
    row = pl.BlockSpec((tm, aw), lambda i: (i, 0))
    cache_rows = pl.BlockSpec((tm * n_heads, hw), lambda i: (i, 0))
    out_specs = [cache_rows, cache_rows, row]
    out_shape = [jax.ShapeDtypeStruct((m * n_heads, hw), F32),
                 jax.ShapeDtypeStruct((m * n_heads, hw), F32),
                 jax.ShapeDtypeStruct((m, aw), BF16)]
    if att_tile is not None:
        assert tm % att_tile == 0
        tiles = tm // att_tile
        out_specs += [row, pl.BlockSpec((tiles, n_heads, hw, att_tile), lambda i: (i, 0, 0, 0))]
        out_shape += [jax.ShapeDtypeStruct((m, aw), BF16),
                      jax.ShapeDtypeStruct((m // att_tile, n_heads, hw, att_tile), BF16)]
    return pl.pallas_call(
        functools.partial(_qkv_rope_kernel, n_heads=n_heads, att_tile=att_tile),
        grid=(m // tm,),
        in_specs=[pl.BlockSpec((tm, d), lambda i: (i, 0)), _const_spec(w_qkv.shape),
                  tab_spec, tab_spec],
        out_specs=out_specs,
        out_shape=out_shape,
        compiler_params=_params("parallel"),
        name="qkv_rope",
    )(x, w_qkv, cos, sin)


def _diff_lambda(lq1, lk1, lq2, lk2):
    s1 = jnp.sum(lq1 * lk1, axis=-1, keepdims=True)
    s2 = jnp.sum(lq2 * lk2, axis=-1, keepdims=True)
    return jnp.exp(s1) - jnp.exp(s2) + LAMBDA_INIT


def _sub_norm(o, g):
    return o * lax.rsqrt(jnp.mean(o * o, axis=-1, keepdims=True) + LN_EPS) * g * (1.0 - LAMBDA_INIT)


def _attn_prompt_kernel(q_ref, k_ref, vt_ref, lq1_ref, lk1_ref, lq2_ref, lk2_ref, sg_ref, o_ref,
                        q2_ref, s_ref, m_ref, l_ref, acc_ref, *, t, heads, qc):
    i = pl.program_id(2)
    hw = 2 * HEAD_DIM
    lane = lax.broadcasted_iota(jnp.int32, (t, hw), 1)
    for g in range(heads):
        q = q_ref[:, g * hw:(g + 1) * hw]
        zero = jnp.zeros_like(q)
        q2_ref[g, 0:t, :] = jnp.where(lane < HEAD_DIM, q, zero)
        q2_ref[g, t:2 * t, :] = jnp.where(lane >= HEAD_DIM, q, zero)
        m_ref[g] = jnp.full((1, 2 * t), NEG_INF, F32)
        l_ref[g] = jnp.zeros((1, 2 * t), F32)
        acc_ref[g] = jnp.zeros((hw, 2 * t), F32)

    chains = [(g, c) for g in range(heads) for c in range(2 * t // qc)]
    ahead = min(ATT_LOOKAHEAD, len(chains))

    def scores(off, g, c):
        kt = k_ref[pl.ds(off, t), g * hw:(g + 1) * hw]
        return _dot_nt(kt, q2_ref[g, c * qc:(c + 1) * qc, :])

    def step(j, masked):
        off = pl.multiple_of(j * t, t)
        for idx, (g, c) in enumerate(chains):
            nxt = idx + ahead
            if nxt < len(chains):
                s_ref[nxt] = scores(off, *chains[nxt])
            elif not masked:
                s_ref[nxt - len(chains)] = scores(pl.multiple_of(off + t, t), *chains[nxt - len(chains)])
            cs = slice(c * qc, (c + 1) * qc)
            s = s_ref[idx]
            if masked:
                key_pos = lax.broadcasted_iota(jnp.int32, (t, qc), 0)
                qry_pos = lax.broadcasted_iota(jnp.int32, (t, qc), 1) + (c * qc) % t
                s = jnp.where(key_pos <= qry_pos, s, NEG_INF)
            m_old = m_ref[g, :, cs]
            m_new = jnp.maximum(m_old, jnp.max(s, axis=0, keepdims=True))
            alpha = jnp.exp2(m_old - m_new)
            p = jnp.exp2(s - m_new)
            l_ref[g, :, cs] = alpha * l_ref[g, :, cs] + jnp.sum(p, axis=0, keepdims=True)
            m_ref[g, :, cs] = m_new
            acc_ref[g, :, cs] = alpha * acc_ref[g, :, cs] + _dot(vt_ref[j, g], p.astype(BF16))

    def body(j, carry):
        step(j, False)
        return carry

    for idx in range(ahead):
        s_ref[idx] = scores(0, *chains[idx])
    lax.fori_loop(0, i, body, 0)
    step(i, True)

    lam = _diff_lambda(lq1_ref[...], lk1_ref[...], lq2_ref[...], lk2_ref[...])
    for g in range(heads):
        a = acc_ref[g] * (1.0 / l_ref[g])
        o = (a[:, 0:t] - lam * a[:, t:2 * t]).T
        o_ref[:, g * hw:(g + 1) * hw] = _sub_norm(o, sg_ref[...]).astype(o_ref.dtype)


def _attn_prompt(qb, kb, vt, lam_p, subln_g, bsz, t_seq, n_heads, t):
    m, aw = qb.shape
    hw = aw // n_heads
    heads = min(ATT_HEADS, n_heads)
    assert t_seq % t == 0 and n_heads % heads == 0
    nq = t_seq // t
    vec = _const_spec((1, HEAD_DIM))
    qc = math.gcd(ATT_QCHUNK, t)
    return pl.pallas_call(
        functools.partial(_attn_prompt_kernel, t=t, heads=heads, qc=qc),
        grid=(bsz, n_heads // heads, nq),
        in_specs=[
            pl.BlockSpec((t, heads * hw), lambda b, h, i: (b * nq + i, h)),
            pl.BlockSpec((t_seq, heads * hw), lambda b, h, i: (b, h)),
            pl.BlockSpec((nq, heads, hw, t), lambda b, h, i: (b, h, 0, 0)),
            vec, vec, vec, vec, _const_spec((1, hw)),
        ],
        out_specs=pl.BlockSpec((t, heads * hw), lambda b, h, i: (b * nq + i, h)),
        out_shape=jax.ShapeDtypeStruct((m, aw), BF16),
        scratch_shapes=[
            pltpu.VMEM((heads, 2 * t, hw), BF16),
            pltpu.VMEM((heads * (2 * t // qc), t, qc), F32),
            pltpu.VMEM((heads, 1, 2 * t), F32),
            pltpu.VMEM((heads, 1, 2 * t), F32),
            pltpu.VMEM((heads, hw, 2 * t), F32),
        ],
        compiler_params=_params("parallel", "parallel", "arbitrary"),
        name="attn_prompt",
    )(qb, kb, vt, *lam_p, subln_g)


def _attn_sample_kernel(pt_ref, q_ref, kn_ref, vn_ref, lq1_ref, lk1_ref, lq2_ref, lk2_ref, sg_ref,
                        *refs, n_pages, n_heads):
    k_refs = refs[:n_pages]
    v_refs = refs[n_pages:2 * n_pages]
    o_ref = refs[2 * n_pages]
    qm_ref, new_ref, s_ref, m_ref, l_ref, acc_ref = refs[2 * n_pages + 1:]
    p_idx = pl.program_id(1)
    hw = 2 * HEAD_DIM
    rows = 2 * n_heads
    page_rows = k_refs[0].shape[0]
    lane = lax.broadcasted_iota(jnp.int32, (1, hw), 1)

    @pl.when(p_idx == 0)
    def _():
        q = q_ref[...].astype(F32)
        for h in range(n_heads):
            qh = q[:, h * hw:(h + 1) * hw]
            qm_ref[h:h + 1, :] = jnp.where(lane < HEAD_DIM, qh, 0.0)
            qm_ref[n_heads + h:n_heads + h + 1, :] = jnp.where(lane >= HEAD_DIM, qh, 0.0)
        m_ref[...] = jnp.full(m_ref.shape, NEG_INF, F32)
        l_ref[...] = jnp.zeros(l_ref.shape, F32)
        acc_ref[...] = jnp.zeros(acc_ref.shape, F32)

    qm = qm_ref[...].astype(BF16)
    own = (lax.broadcasted_iota(jnp.int32, (rows, page_rows), 0) % n_heads
           == lax.broadcasted_iota(jnp.int32, (rows, page_rows), 1) % n_heads)
    for pg in range(n_pages):
        s_pg = _dot_nt(qm, k_refs[pg][...].astype(BF16))
        s_ref[:, pg * page_rows:(pg + 1) * page_rows] = jnp.where(own, s_pg, NEG_INF)
    s = s_ref[...]
    m_old = m_ref[...]
    m_new = jnp.maximum(m_old, jnp.max(s, axis=-1, keepdims=True))
    alpha = jnp.exp2(m_old - m_new)
    p = jnp.exp2(s - m_new)
    l_ref[...] = alpha * l_ref[...] + jnp.sum(p, axis=-1, keepdims=True)
    m_ref[...] = m_new
    pv = None
    for pg in range(n_pages):
        d = _dot(p[:, pg * page_rows:(pg + 1) * page_rows].astype(BF16), v_refs[pg][...].astype(BF16))
        pv = d if pv is None else pv + d
    acc_ref[...] = alpha * acc_ref[...] + pv

    @pl.when(p_idx == pl.num_programs(1) - 1)
    def _():
        lam = _diff_lambda(lq1_ref[...], lk1_ref[...], lq2_ref[...], lk2_ref[...])
        kn = kn_ref[...].astype(BF16).astype(F32)
        vn = vn_ref[...].astype(BF16).astype(F32)
        for c in range(2):
            new_ref[0, c * n_heads:(c + 1) * n_heads, :] = kn
            new_ref[1, c * n_heads:(c + 1) * n_heads, :] = vn
        s_new = jnp.sum(qm_ref[...] * new_ref[0], axis=-1, keepdims=True)
        m_old = m_ref[...]
        m_new = jnp.maximum(m_old, s_new)
        alpha = jnp.exp2(m_old - m_new)
        p_new = jnp.exp2(s_new - m_new)
        l = alpha * l_ref[...] + p_new
        acc = alpha * acc_ref[...] + p_new.astype(BF16).astype(F32) * new_ref[1]
        o = acc / l
        o = _sub_norm(o[0:n_heads, :] - lam * o[n_heads:rows, :], sg_ref[...])
        for h in range(n_heads):
            o_ref[:, h * hw:(h + 1) * hw] = o[h:h + 1, :].astype(o_ref.dtype)


def _attn_sample(qb, k_new, v_new, lam_p, subln_g, cache_k, cache_v, page_table, n_heads):
    n, aw = qb.shape
    hw = aw // n_heads
    page = cache_k.shape[1]
    n_log = page_table.shape[1]
    pp = DEC_PAGES
    while n_log % pp:
        pp //= 2
    row3 = lambda a: a.reshape(n, 1, aw)
    row_spec = pl.BlockSpec((None, 1, aw), lambda b, p, pt: (b, 0, 0))
    new_spec = pl.BlockSpec((None, n_heads, hw), lambda b, p, pt: (b, 0, 0))
    vec = pl.BlockSpec((1, HEAD_DIM), lambda b, p, pt: (0, 0))

    def page_spec(j):
        return pl.BlockSpec((None, page * n_heads, hw), lambda b, p, pt: (pt[b, p * pp + j], 0, 0))

    grid_spec = pltpu.PrefetchScalarGridSpec(
        num_scalar_prefetch=1,
        grid=(n, n_log // pp),
        in_specs=[row_spec, new_spec, new_spec, vec, vec, vec, vec,
                  pl.BlockSpec((1, hw), lambda b, p, pt: (0, 0))]
                 + [page_spec(j) for j in range(pp)] + [page_spec(j) for j in range(pp)],
        out_specs=row_spec,
        scratch_shapes=[
            pltpu.VMEM((2 * n_heads, hw), F32),
            pltpu.VMEM((2, 2 * n_heads, hw), F32),
            pltpu.VMEM((2 * n_heads, pp * page * n_heads), F32),
            pltpu.VMEM((2 * n_heads, 1), F32),
            pltpu.VMEM((2 * n_heads, 1), F32),
            pltpu.VMEM((2 * n_heads, hw), F32),
        ],
    )
    ck = cache_k.reshape(cache_k.shape[0], page * n_heads, hw)
    cv = cache_v.reshape(cache_v.shape[0], page * n_heads, hw)
    out = pl.pallas_call(
        functools.partial(_attn_sample_kernel, n_pages=pp, n_heads=n_heads),
        grid_spec=grid_spec,
        out_shape=jax.ShapeDtypeStruct((n, 1, aw), BF16),
        compiler_params=_params("parallel", "arbitrary"),
        name="attn_sample",
    )(page_table, row3(qb), k_new.reshape(n, n_heads, hw), v_new.reshape(n, n_heads, hw),
      *lam_p, subln_g, *([ck] * pp), *([cv] * pp))
    return out.reshape(n, aw)


def _proj_ln_kernel(a_ref, x_ref, w_ref, g_ref, b_ref, o_ref):
    y = _dot(a_ref[...], w_ref[...])
    o_ref[...] = _layer_norm(ALPHA * x_ref[...] + y, g_ref[...], b_ref[...])


def _proj_ln(a, x, w, g, b):
    m, d = x.shape
    tm = min(PROJ_ROWS, m)
    return pl.pallas_call(
        _proj_ln_kernel,
        grid=(pl.cdiv(m, tm),),
        in_specs=[pl.BlockSpec((tm, a.shape[1]), lambda i: (i, 0)),
                  pl.BlockSpec((tm, d), lambda i: (i, 0)),
                  _const_spec(w.shape), _const_spec((1, d)), _const_spec((1, d))],
        out_specs=pl.BlockSpec((tm, d), lambda i: (i, 0)),
        out_shape=jax.ShapeDtypeStruct((m, d), F32),
        compiler_params=_params("parallel"),
        name="proj_ln",
    )(a, x, w, g, b)


def _block_diag(w):
    n, bi, bj = w.shape
    eye = jnp.eye(n, dtype=w.dtype)
    return jnp.einsum("nij,nm->nimj", w, eye).reshape(n * bi, n * bj)


def kernel(x_prompt, x_sample, state_lru_h, state_conv_a, state_conv_b, cache_k, cache_v, page_table,
           ln_g, ln_b, ffn_w_gate, ffn_w_up, ffn_w_down,
           rec_w_in, conv_a_w, conv_a_b, lru_w_a, lru_b_a, lru_w_x, lru_b_x, lru_lambda,
           conv_b_w, conv_b_b, conv_ln_g, conv_ln_b, rec_w_out,
           att_w_qkv, lambda_q1, lambda_k1, lambda_q2, lambda_k2, subln_g, att_w_out):
    bp, tp, d = x_prompt.shape
    bs, ts, _ = x_sample.shape
    assert ts == 1, "the sample group decodes one token per sequence"
    n_heads = cache_k.shape[2]
    d_a = state_lru_h.shape[-1]
    d_ff = ffn_w_gate.shape[-1]
    assert d_ff % FFN_CHUNK == 0
    row = lambda v: v.reshape(1, -1)

    ffn_w = (ffn_w_gate.astype(BF16), ffn_w_up.astype(BF16), ffn_w_down.astype(BF16))
    lng = [[row(ln_g[l, i]) for i in range(3)] for l in range(DEPTH)]
    lnb = [[row(ln_b[l, i]) for i in range(3)] for l in range(DEPTH)]
    mp = dict(
        w_in=rec_w_in.astype(BF16), conv_a_w=conv_a_w, conv_a_b=row(conv_a_b),
        wa=_block_diag(lru_w_a).astype(BF16), ba=row(lru_b_a),
        wx=_block_diag(lru_w_x).astype(BF16), bx=row(lru_b_x), lam=row(lru_lambda),
        conv_b_w=conv_b_w, conv_b_b=row(conv_b_b), cln_g=row(conv_ln_g), cln_b=row(conv_ln_b),
        w_out=rec_w_out.astype(BF16))
    w_qkv = att_w_qkv.astype(BF16)
    w_o = att_w_out.astype(BF16)
    lam_p = (row(lambda_q1), row(lambda_k1), row(lambda_q2), row(lambda_k2))
    sg = row(subln_g)

    def ffn_sub(x, l, i, proj=None):
        return _ffn_sublayer(x, *ffn_w, l, i, lng[l][2 * i], lnb[l][2 * i], proj)

    x = x_prompt.reshape(bp * tp, d)
    x = ffn_sub(x, 0, 0)
    x, h_p, ca_p, cb_p = _mixer_prompt(
        x.reshape(bp, tp, d), jnp.zeros((bp, d_a), F32),
        jnp.zeros((bp, state_conv_a.shape[1], d_a), F32),
        jnp.zeros((bp, state_conv_b.shape[1], d_a), F32), mp, lng[0][1], lnb[0][1])
    x = ffn_sub(x.reshape(bp * tp, d), 0, 1)
    x = ffn_sub(x, 1, 0)
    cos_p, sin_p = _rope_tables(jnp.arange(tp, dtype=jnp.int32))
    att_tile = min(ATT_TILE, tp)
    k_p, v_p, qb, kb, vt = _qkv_rope(x, w_qkv, cos_p, sin_p, n_heads, tp, att_tile)
    att = _attn_prompt(qb, kb, vt, lam_p, sg, bp, tp, n_heads, att_tile)
    y_prompt = ffn_sub(x, 1, 1, (att, w_o, lng[1][1], lnb[1][1])).reshape(bp, tp, d)

    past = page_table.shape[1] * cache_k.shape[1]
    xs = x_sample.reshape(bs, d)
    xs = ffn_sub(xs, 0, 0)
    xs, h_s, ca_s, cb_s = _mixer_sample(xs, state_lru_h, state_conv_a, state_conv_b, mp,
                                        lng[0][1], lnb[0][1])
    xs = ffn_sub(xs, 0, 1)
    xs = ffn_sub(xs, 1, 0)
    cos_s, sin_s = _rope_tables(past + jnp.arange(ts, dtype=jnp.int32))
    k_s, v_s, qsb = _qkv_rope(xs, w_qkv, cos_s, sin_s, n_heads, ts)
    att_s = _attn_sample(qsb, k_s, v_s, lam_p, sg, cache_k, cache_v, page_table, n_heads)
    y_sample = ffn_sub(xs, 1, 1, (att_s, w_o, lng[1][1], lnb[1][1])).reshape(bs, ts, d)

    hw = 2 * HEAD_DIM
    return (y_prompt, y_sample, h_p, ca_p, cb_p,
            k_p.reshape(bp, tp, n_heads, hw), v_p.reshape(bp, tp, n_heads, hw),
            h_s, ca_s, cb_s,
            k_s.reshape(bs, ts, n_heads, hw), v_s.reshape(bs, ts, n_heads, hw))
```

```python
import functools
import math

import jax
import jax.numpy as jnp
from jax import lax
from jax.experimental import pallas as pl
from jax.experimental.pallas import tpu as pltpu

F32 = jnp.float32
BF16 = jnp.bfloat16

DEPTH = 2
LRU_BLOCKS = 8
LRU_C = 8.0
HEAD_DIM = 64
ROPE_THETA = 10000.0
ATT_SCALE = HEAD_DIM ** -0.5
LOG2_E = math.log2(math.e)
NEG_INF = -1e30
ALPHA = (2 * DEPTH) ** 0.25
LN_EPS = 1e-5
LAMBDA_INIT = 0.8 - 0.6 * math.exp(-0.3 * 1)

LANES = 128
SUBLANES = 8
VMEM_LIMIT_BYTES = 56 * 1024 * 1024

FFN_CHUNK = 256
FFN_ROWS = 512
FFN_STREAM_MAX_ROWS = 64
MIX_ROWS = 256
FUSED_MIX_ROWS = 128
FUSED_MAX_PAGES = 16
QKV_ROWS = 512
ATT_TILE = 512
ATT_HEADS = 4
ATT_QCHUNK = 512
ATT_LOOKAHEAD = 1
PROJ_ROWS = 512
DEC_PAGES = 8


def _layer_norm(y, g, b):
    mu = jnp.mean(y, axis=-1, keepdims=True)
    d = y - mu
    var = jnp.mean(d * d, axis=-1, keepdims=True)
    return d * lax.rsqrt(var + LN_EPS) * g + b


def _sigmoid(x):
    return 1.0 / (1.0 + jnp.exp(-x))


def _dot(a, b):
    return jnp.dot(a, b, preferred_element_type=F32)


def _dot_nt(a, b):
    return lax.dot_general(a, b, (((1,), (1,)), ((), ())), preferred_element_type=F32)


def _const_spec(shape):
    nd = len(shape)
    return pl.BlockSpec(shape, lambda *_: (0,) * nd)


def _resident_spec(shape):
    nd = len(shape)
    return pl.BlockSpec(shape, lambda *_: (0,) * nd, pipeline_mode=pl.Buffered(1))


def _params(*sem):
    return pltpu.CompilerParams(dimension_semantics=sem, vmem_limit_bytes=VMEM_LIMIT_BYTES)


def _swiglu(xb, wg_ref, wu_ref, wd_ref):
    d_ff = wg_ref.shape[1]
    acc = None
    for c in range(d_ff // FFN_CHUNK):
        cols = slice(c * FFN_CHUNK, (c + 1) * FFN_CHUNK)
        g = _dot(xb, wg_ref[:, cols])
        u = _dot(xb, wu_ref[:, cols])
        d = _dot((g * _sigmoid(g) * u).astype(BF16), wd_ref[cols, :])
        acc = d if acc is None else acc + d
    return acc


def _ffn_kernel(*refs, with_proj):
    if with_proj:
        a_ref, xr_ref, wp_ref, gp_ref, bp_ref, wg_ref, wu_ref, wd_ref, g_ref, b_ref, o_ref = refs
        x = _layer_norm(ALPHA * xr_ref[...] + _dot(a_ref[...], wp_ref[...]), gp_ref[...], bp_ref[...])
    else:
        x_ref, wg_ref, wu_ref, wd_ref, g_ref, b_ref, o_ref = refs
        x = x_ref[...]
    acc = _swiglu(x.astype(BF16), wg_ref, wu_ref, wd_ref)
    o_ref[...] = _layer_norm(ALPHA * x + 0.5 * acc, g_ref[...], b_ref[...])


def _ffn_stream_kernel(x_ref, wg_ref, wu_ref, wd_ref, g_ref, b_ref, o_ref, acc_ref):
    c = pl.program_id(0)
    x = x_ref[...]
    xb = x.astype(BF16)
    g = _dot(xb, wg_ref[...])
    u = _dot(xb, wu_ref[...])
    d = _dot((g * _sigmoid(g) * u).astype(BF16), wd_ref[...])

    @pl.when(c == 0)
    def _():
        acc_ref[...] = d

    @pl.when(c > 0)
    def _():
        acc_ref[...] += d

    @pl.when(c == pl.num_programs(0) - 1)
    def _():
        o_ref[...] = _layer_norm(ALPHA * x + 0.5 * acc_ref[...], g_ref[...], b_ref[...])


def _ffn_sublayer(x, wg, wu, wd, layer, idx, g, b, proj=None):
    m, d = x.shape
    d_ff = wg.shape[-1]
    fc = FFN_CHUNK
    assert d_ff % fc == 0
    if m <= FFN_STREAM_MAX_ROWS:
        if proj is not None:
            x = _proj_ln(proj[0], x, *proj[1:])
        return pl.pallas_call(
            _ffn_stream_kernel,
            grid=(d_ff // fc,),
            in_specs=[
                _const_spec((m, d)),
                pl.BlockSpec((None, None, d, fc), lambda c: (layer, idx, 0, c)),
                pl.BlockSpec((None, None, d, fc), lambda c: (layer, idx, 0, c)),
                pl.BlockSpec((None, None, fc, d), lambda c: (layer, idx, c, 0)),
                _const_spec((1, d)),
                _const_spec((1, d)),
            ],
            out_specs=_const_spec((m, d)),
            out_shape=jax.ShapeDtypeStruct((m, d), F32),
            scratch_shapes=[pltpu.VMEM((m, d), F32)],
            compiler_params=_params("arbitrary"),
            name="ffn_stream",
        )(x, wg, wu, wd, g, b)
    tm = min(FFN_ROWS, m)
    rows = pl.BlockSpec((tm, d), lambda i: (i, 0))
    if proj is None:
        lead_args, lead_specs = (x,), [rows]
    else:
        a, wp, gp, bp = proj
        lead_args = (a, x, wp, gp, bp)
        lead_specs = [pl.BlockSpec((tm, a.shape[1]), lambda i: (i, 0)), rows,
                      _resident_spec(wp.shape), _const_spec((1, d)), _const_spec((1, d))]
    pick = lambda *_: (layer, idx, 0, 0)
    resident = pl.Buffered(1)
    return pl.pallas_call(
        functools.partial(_ffn_kernel, with_proj=proj is not None),
        grid=(pl.cdiv(m, tm),),
        in_specs=lead_specs + [
            pl.BlockSpec((None, None, d, d_ff), pick, pipeline_mode=resident),
            pl.BlockSpec((None, None, d, d_ff), pick, pipeline_mode=resident),
            pl.BlockSpec((None, None, d_ff, d), pick, pipeline_mode=resident),
            _const_spec((1, d)),
            _const_spec((1, d)),
        ],
        out_specs=rows,
        out_shape=jax.ShapeDtypeStruct((m, d), F32),
        compiler_params=_params("parallel"),
        name="ffn_sublayer",
    )(*lead_args, wg, wu, wd, g, b)


def _softplus(x):
    return jnp.maximum(x, 0.0) + jnp.log1p(jnp.exp(-jnp.abs(x)))


def _gelu_tanh(x):
    c = math.sqrt(2.0 / math.pi)
    return 0.5 * x * (1.0 + jnp.tanh(c * (x + 0.044715 * (x * x * x))))


def _lru_gates(xa_c, wa, ba, wx, bx, lam):
    xb = xa_c.astype(BF16)
    r = _sigmoid(_dot(xb, wa) + ba)
    i = _sigmoid(_dot(xb, wx) + bx)
    log_a = (-LRU_C) * r * _softplus(-lam)
    a = jnp.exp(log_a)
    th = jnp.tanh(log_a)
    y = -2.0 * th / (1.0 - th)
    u = jnp.where(y > 0.0, y * lax.rsqrt(y), 0.0) * (i * xa_c)
    return a, u


def _conv_b_post(acc, g, b):
    y = _layer_norm(acc, g, b)
    return y * _sigmoid(y)


HIST_A = 8
HIST_B = 32


N_MIXER_IN = 19


def _mixer_prompt_kernel(*refs, tm, pad, ka, kb):
    _mixer_tile(pl.program_id(1) == 0, *refs, tm=tm, pad=pad, ka=ka, kb=kb)


def _mixer_tile(first, x_ref, h0_ref, ca0_ref, cb0_ref, w_in_ref, caw_ref, cab_ref,
                wa_ref, ba_ref, wx_ref, bx_ref, lam_ref, cbw_ref, cbb_ref,
                clg_ref, clb_ref, wout_ref, g_ref, b_ref,
                y_ref, hl_ref, cas_ref, cbs_ref,
                ha_ref, hb_ref, sh_ref, sa_ref, su_ref, hc_ref, *, tm, pad, ka, kb):
    d_a = ha_ref.shape[1]

    @pl.when(first)
    def _():
        ha_ref[HIST_A - (ka - 1):HIST_A, :] = ca0_ref[...]
        hb_ref[HIST_B - (kb - 1):HIST_B, :] = cb0_ref[...]
        hc_ref[...] = h0_ref[...]
        sa_ref[0:pad, :] = jnp.ones((pad, d_a), F32)
        su_ref[0:pad, :] = jnp.zeros((pad, d_a), F32)

    x = x_ref[...]
    z = _dot(x.astype(BF16), w_in_ref[...])
    gate_a = z[:, 0:d_a]
    xa = z[:, d_a:2 * d_a]
    vb = z[:, 2 * d_a:3 * d_a]
    gb = z[:, 3 * d_a:4 * d_a]

    ha_ref[HIST_A:HIST_A + tm, :] = xa
    xa_c = cab_ref[...]
    for j in range(ka):
        s = HIST_A - (ka - 1) + j
        xa_c = xa_c + caw_ref[j:j + 1, :] * ha_ref[s:s + tm, :]
    a, u = _lru_gates(xa_c, wa_ref[...], ba_ref[...], wx_ref[...], bx_ref[...], lam_ref[...])
    sa_ref[pad:pad + tm, :] = a
    su_ref[pad:pad + tm, :] = u
    su_ref[pad:pad + 1, :] = u[0:1, :] + a[0:1, :] * hc_ref[...]
    k = 1
    while k < tm:
        a_cur = sa_ref[pad:pad + tm, :]
        u_cur = su_ref[pad:pad + tm, :]
        a_sh = sa_ref[pad - k:pad - k + tm, :]
        u_sh = su_ref[pad - k:pad - k + tm, :]
        su_ref[pad:pad + tm, :] = u_cur + a_cur * u_sh
        if 2 * k < tm:
            sa_ref[pad:pad + tm, :] = a_cur * a_sh
        k *= 2
    h_seq = su_ref[pad:pad + tm, :]
    h_last = h_seq[tm - 1:tm, :]
    hc_ref[...] = h_last
    hl_ref[...] = h_last
    ya = h_seq * _gelu_tanh(gate_a)

    hb_ref[HIST_B:HIST_B + tm, :] = vb * _sigmoid(gb)
    span = HIST_B - SUBLANES + tm
    acc = cbb_ref[...]
    for r in range(SUBLANES):
        taps = [j for j in range(kb) if (HIST_B - (kb - 1) + j) % SUBLANES == r]
        if not taps:
            continue
        if r:
            sh_ref[r - 1] = hb_ref[r:r + span, :]
        for j in taps:
            s = HIST_B - (kb - 1) + j - r
            rows = sh_ref[r - 1, s:s + tm, :] if r else hb_ref[s:s + tm, :]
            acc = acc + cbw_ref[j:j + 1, :] * rows
    yb = _conv_b_post(acc, clg_ref[...], clb_ref[...])

    m = _dot(ya.astype(BF16), wout_ref[0:d_a, :]) + _dot(yb.astype(BF16), wout_ref[d_a:2 * d_a, :])
    y_ref[...] = _layer_norm(ALPHA * x + m, g_ref[...], b_ref[...])

    tail_a = ha_ref[tm + HIST_A - (ka - 1):tm + HIST_A, :]
    cas_ref[...] = tail_a
    ha_ref[HIST_A - (ka - 1):HIST_A, :] = tail_a
    tail_b = hb_ref[tm + HIST_B - (kb - 1):tm + HIST_B, :]
    cbs_ref[...] = tail_b
    hb_ref[HIST_B - (kb - 1):HIST_B, :] = tail_b


def _mixer_prompt(x, h0, ca0, cb0, mp, g, b):
    bsz, t, d = x.shape
    d_a = h0.shape[-1]
    ka = ca0.shape[1] + 1
    kb = cb0.shape[1] + 1
    assert ka - 1 <= HIST_A and kb - 1 <= HIST_B
    tm = min(MIX_ROWS, t)
    assert t % tm == 0
    pad = max(SUBLANES, tm // 2)
    kernel = functools.partial(_mixer_prompt_kernel, tm=tm, pad=pad, ka=ka, kb=kb)
    bt = lambda bi, ti: (bi, ti, 0)
    b0 = lambda bi, ti: (bi, 0, 0)
    outs = pl.pallas_call(
        kernel,
        grid=(bsz, t // tm),
        in_specs=[
            pl.BlockSpec((None, tm, d), bt),
            pl.BlockSpec((None, 1, d_a), b0),
            pl.BlockSpec((None, ka - 1, d_a), b0),
            pl.BlockSpec((None, kb - 1, d_a), b0),
            _const_spec(mp["w_in"].shape),
            _const_spec((ka, d_a)), _const_spec((1, d_a)),
            _const_spec((d_a, d_a)), _const_spec((1, d_a)),
            _const_spec((d_a, d_a)), _const_spec((1, d_a)),
            _const_spec((1, d_a)),
            _const_spec((kb, d_a)), _const_spec((1, d_a)),
            _const_spec((1, d_a)), _const_spec((1, d_a)),
            _const_spec(mp["w_out"].shape),
            _const_spec((1, d)), _const_spec((1, d)),
        ],
        out_specs=[
            pl.BlockSpec((None, tm, d), bt),
            pl.BlockSpec((None, 1, d_a), b0),
            pl.BlockSpec((None, ka - 1, d_a), b0),
            pl.BlockSpec((None, kb - 1, d_a), b0),
        ],
        out_shape=[
            jax.ShapeDtypeStruct((bsz, t, d), F32),
            jax.ShapeDtypeStruct((bsz, 1, d_a), F32),
            jax.ShapeDtypeStruct((bsz, ka - 1, d_a), F32),
            jax.ShapeDtypeStruct((bsz, kb - 1, d_a), F32),
        ],
        scratch_shapes=[
            pltpu.VMEM((HIST_A + tm, d_a), F32),
            pltpu.VMEM((HIST_B + tm, d_a), F32),
            pltpu.VMEM((SUBLANES - 1, HIST_B - SUBLANES + tm, d_a), F32),
            pltpu.VMEM((pad + tm, d_a), F32),
            pltpu.VMEM((pad + tm, d_a), F32),
            pltpu.VMEM((1, d_a), F32),
        ],
        compiler_params=_params("parallel", "arbitrary"),
        name="mixer_prompt",
    )(x, h0.reshape(bsz, 1, d_a), ca0, cb0, mp["w_in"], mp["conv_a_w"], mp["conv_a_b"],
      mp["wa"], mp["ba"], mp["wx"], mp["bx"], mp["lam"], mp["conv_b_w"], mp["conv_b_b"],
      mp["cln_g"], mp["cln_b"], mp["w_out"], g, b)
    y, hl, cas, cbs = outs
    return y, hl.reshape(bsz, d_a), cas, cbs


def _mixer_sample_kernel(x_ref, h0_ref, ca_ref, cb_ref, w_in_ref, caw_ref, cab_ref,
                         wa_ref, ba_ref, wx_ref, bx_ref, lam_ref, cbw_ref, cbb_ref,
                         clg_ref, clb_ref, wout_ref, g_ref, b_ref,
                         y_ref, hl_ref, cas_ref, cbs_ref, *, ka, kb):
    d_a = h0_ref.shape[1]
    x = x_ref[...]
    z = _dot(x.astype(BF16), w_in_ref[...])
    gate_a = z[:, 0:d_a]
    xa = z[:, d_a:2 * d_a]
    vb = z[:, 2 * d_a:3 * d_a]
    gb = z[:, 3 * d_a:4 * d_a]

    xa_c = cab_ref[...]
    for j in range(ka - 1):
        xa_c = xa_c + caw_ref[j:j + 1, :] * ca_ref[j]
    xa_c = xa_c + caw_ref[ka - 1:ka, :] * xa
    a, u = _lru_gates(xa_c, wa_ref[...], ba_ref[...], wx_ref[...], bx_ref[...], lam_ref[...])
    h = a * h0_ref[...] + u
    hl_ref[...] = h
    ya = h * _gelu_tanh(gate_a)

    ub = vb * _sigmoid(gb)
    acc = cbb_ref[...]
    for j in range(kb - 1):
        acc = acc + cbw_ref[j:j + 1, :] * cb_ref[j]
    acc = acc + cbw_ref[kb - 1:kb, :] * ub
    yb = _conv_b_post(acc, clg_ref[...], clb_ref[...])

    m = _dot(ya.astype(BF16), wout_ref[0:d_a, :]) + _dot(yb.astype(BF16), wout_ref[d_a:2 * d_a, :])
    y_ref[...] = _layer_norm(ALPHA * x + m, g_ref[...], b_ref[...])

    for j in range(ka - 2):
        cas_ref[j] = ca_ref[j + 1]
    cas_ref[ka - 2] = xa
    for j in range(kb - 2):
        cbs_ref[j] = cb_ref[j + 1]
    cbs_ref[kb - 2] = ub


def _mixer_sample(x, h0, ca0, cb0, mp, g, b):
    n, d = x.shape
    d_a = h0.shape[-1]
    ka = ca0.shape[1] + 1
    kb = cb0.shape[1] + 1
    ca_t = jnp.swapaxes(ca0, 0, 1)
    cb_t = jnp.swapaxes(cb0, 0, 1)
    args = (x, h0, ca_t, cb_t, mp["w_in"], mp["conv_a_w"], mp["conv_a_b"],
            mp["wa"], mp["ba"], mp["wx"], mp["bx"], mp["lam"], mp["conv_b_w"], mp["conv_b_b"],
            mp["cln_g"], mp["cln_b"], mp["w_out"], g, b)
    y, hl, cas, cbs = pl.pallas_call(
        functools.partial(_mixer_sample_kernel, ka=ka, kb=kb),
        grid=(1,),
        in_specs=[_const_spec(a.shape) for a in args],
        out_specs=[_const_spec((n, d)), _const_spec((n, d_a)),
                   _const_spec((ka - 1, n, d_a)), _const_spec((kb - 1, n, d_a))],
        out_shape=[
            jax.ShapeDtypeStruct((n, d), F32),
            jax.ShapeDtypeStruct((n, d_a), F32),
            jax.ShapeDtypeStruct((ka - 1, n, d_a), F32),
            jax.ShapeDtypeStruct((kb - 1, n, d_a), F32),
        ],
        compiler_params=_params("arbitrary"),
        name="mixer_sample",
    )(*args)
    return y, hl, jnp.swapaxes(cas, 0, 1), jnp.swapaxes(cbs, 0, 1)


def _rope_tables(pos):
    half = HEAD_DIM // 2
    inv = ROPE_THETA ** (-jnp.arange(half, dtype=F32) / half)
    ang = pos.astype(F32)[:, None] * inv[None, :]
    cos = jnp.cos(ang)
    sin = jnp.sin(ang)
    c = jnp.concatenate([cos, cos, cos, cos], axis=-1)
    s = jnp.concatenate([-sin, sin, -sin, sin], axis=-1)
    return c, s


def _qkv_rope_kernel(x_ref, w_ref, cos_ref, sin_ref, k_ref, v_ref, qb_ref, *tile_refs,
                     n_heads, att_tile):
    aw = n_heads * 2 * HEAD_DIM
    tm = x_ref.shape[0]
    z = _dot(x_ref[...].astype(BF16), w_ref[...])
    c = cos_ref[...]
    s = sin_ref[...]
    hw = 2 * HEAD_DIM
    lane = lax.broadcasted_iota(jnp.int32, (1, hw), 1)
    first_half = (lane % HEAD_DIM) < (HEAD_DIM // 2)

    def rot(xh):
        swapped = jnp.where(first_half, pltpu.roll(xh, hw - HEAD_DIM // 2, axis=1),
                            pltpu.roll(xh, HEAD_DIM // 2, axis=1))
        return xh * c + swapped * s

    for h in range(n_heads):
        sl = slice(h * hw, (h + 1) * hw)
        qh = rot(z[:, h * hw:(h + 1) * hw])
        kh = rot(z[:, aw + h * hw:aw + (h + 1) * hw])
        vh = z[:, 2 * aw + h * hw:2 * aw + (h + 1) * hw]
        qb_ref[:, sl] = (qh * (ATT_SCALE * LOG2_E)).astype(BF16)
        k_ref[pl.ds(h, tm, stride=n_heads), :] = kh
        v_ref[pl.ds(h, tm, stride=n_heads), :] = vh
        if att_tile is not None:
            kb_ref, vt_ref = tile_refs
            kb_ref[:, sl] = kh.astype(BF16)
            for ti in range(vt_ref.shape[0]):
                vt_ref[ti, h] = vh[ti * att_tile:(ti + 1) * att_tile, :].T.astype(BF16)


def _qkv_rope(x, w_qkv, cos, sin, n_heads, t_per_seq, att_tile=None):
    m, d = x.shape
    aw = n_heads * 2 * HEAD_DIM
    hw = 2 * HEAD_DIM
    if t_per_seq == 1:
        tm = m
        tab_spec = pl.BlockSpec((1, hw), lambda i: (0, 0))
    else:
        tm = min(QKV_ROWS, t_per_seq)
        assert t_per_seq % tm == 0
        nt = t_per_seq // tm
        tab_spec = pl.BlockSpec((tm, hw), lambda i: (i % nt, 0))
    row = pl.BlockSpec((tm, aw), lambda i: (i, 0))
    cache_rows = pl.BlockSpec((tm * n_heads, hw), lambda i: (i, 0))
    out_specs = [cache_rows, cache_rows, row]
    out_shape = [jax.ShapeDtypeStruct((m * n_heads, hw), F32),
                 jax.ShapeDtypeStruct((m * n_heads, hw), F32),
                 jax.ShapeDtypeStruct((m, aw), BF16)]
    if att_tile is not None:
        assert tm % att_tile == 0
        tiles = tm // att_tile
        out_specs += [row, pl.BlockSpec((tiles, n_heads, hw, att_tile), lambda i: (i, 0, 0, 0))]
        out_shape += [jax.ShapeDtypeStruct((m, aw), BF16),
                      jax.ShapeDtypeStruct((m // att_tile, n_heads, hw, att_tile), BF16)]
    return pl.pallas_call(
        functools.partial(_qkv_rope_kernel, n_heads=n_heads, att_tile=att_tile),
        grid=(m // tm,),
        in_specs=[pl.BlockSpec((tm, d), lambda i: (i, 0)), _const_spec(w_qkv.shape),
                  tab_spec, tab_spec],
        out_specs=out_specs,
        out_shape=out_shape,
        compiler_params=_params("parallel"),
        name="qkv_rope",
    )(x, w_qkv, cos, sin)


def _diff_lambda(lq1, lk1, lq2, lk2):
    s1 = jnp.sum(lq1 * lk1, axis=-1, keepdims=True)
    s2 = jnp.sum(lq2 * lk2, axis=-1, keepdims=True)
    return jnp.exp(s1) - jnp.exp(s2) + LAMBDA_INIT


def _sub_norm(o, g):
    return o * lax.rsqrt(jnp.mean(o * o, axis=-1, keepdims=True) + LN_EPS) * g * (1.0 - LAMBDA_INIT)


def _attn_prompt_kernel(q_ref, k_ref, vt_ref, lq1_ref, lk1_ref, lq2_ref, lk2_ref, sg_ref, o_ref,
                        q2_ref, s_ref, m_ref, l_ref, acc_ref, *, t, heads, qc):
    i = pl.program_id(2)
    hw = 2 * HEAD_DIM
    lane = lax.broadcasted_iota(jnp.int32, (t, hw), 1)
    for g in range(heads):
        q = q_ref[:, g * hw:(g + 1) * hw]
        zero = jnp.zeros_like(q)
        q2_ref[g, 0:t, :] = jnp.where(lane < HEAD_DIM, q, zero)
        q2_ref[g, t:2 * t, :] = jnp.where(lane >= HEAD_DIM, q, zero)
        m_ref[g] = jnp.full((1, 2 * t), NEG_INF, F32)
        l_ref[g] = jnp.zeros((1, 2 * t), F32)
        acc_ref[g] = jnp.zeros((hw, 2 * t), F32)

    chains = [(g, c) for g in range(heads) for c in range(2 * t // qc)]
    ahead = min(ATT_LOOKAHEAD, len(chains))

    def scores(off, g, c):
        kt = k_ref[pl.ds(off, t), g * hw:(g + 1) * hw]
        return _dot_nt(kt, q2_ref[g, c * qc:(c + 1) * qc, :])

    def step(j, masked):
        off = pl.multiple_of(j * t, t)
        for idx, (g, c) in enumerate(chains):
            nxt = idx + ahead
            if nxt < len(chains):
                s_ref[nxt] = scores(off, *chains[nxt])
            elif not masked:
                s_ref[nxt - len(chains)] = scores(pl.multiple_of(off + t, t), *chains[nxt - len(chains)])
            cs = slice(c * qc, (c + 1) * qc)
            s = s_ref[idx]
            if masked:
                key_pos = lax.broadcasted_iota(jnp.int32, (t, qc), 0)
                qry_pos = lax.broadcasted_iota(jnp.int32, (t, qc), 1) + (c * qc) % t
                s = jnp.where(key_pos <= qry_pos, s, NEG_INF)
            m_old = m_ref[g, :, cs]
            m_new = jnp.maximum(m_old, jnp.max(s, axis=0, keepdims=True))
            alpha = jnp.exp2(m_old - m_new)
            p = jnp.exp2(s - m_new)
            l_ref[g, :, cs] = alpha * l_ref[g, :, cs] + jnp.sum(p, axis=0, keepdims=True)
            m_ref[g, :, cs] = m_new
            acc_ref[g, :, cs] = alpha * acc_ref[g, :, cs] + _dot(vt_ref[j, g], p.astype(BF16))

    def body(j, carry):
        step(j, False)
        return carry

    for idx in range(ahead):
        s_ref[idx] = scores(0, *chains[idx])
    lax.fori_loop(0, i, body, 0)
    step(i, True)

    lam = _diff_lambda(lq1_ref[...], lk1_ref[...], lq2_ref[...], lk2_ref[...])
    for g in range(heads):
        a = acc_ref[g] * (1.0 / l_ref[g])
        o = (a[:, 0:t] - lam * a[:, t:2 * t]).T
        o_ref[:, g * hw:(g + 1) * hw] = _sub_norm(o, sg_ref[...]).astype(o_ref.dtype)


def _attn_prompt(qb, kb, vt, lam_p, subln_g, bsz, t_seq, n_heads, t):
    m, aw = qb.shape
    hw = aw // n_heads
    heads = min(ATT_HEADS, n_heads)
    assert t_seq % t == 0 and n_heads % heads == 0
    nq = t_seq // t
    vec = _const_spec((1, HEAD_DIM))
    qc = math.gcd(ATT_QCHUNK, t)
    return pl.pallas_call(
        functools.partial(_attn_prompt_kernel, t=t, heads=heads, qc=qc),
        grid=(bsz, n_heads // heads, nq),
        in_specs=[
            pl.BlockSpec((t, heads * hw), lambda b, h, i: (b * nq + i, h)),
            pl.BlockSpec((t_seq, heads * hw), lambda b, h, i: (b, h)),
            pl.BlockSpec((nq, heads, hw, t), lambda b, h, i: (b, h, 0, 0)),
            vec, vec, vec, vec, _const_spec((1, hw)),
        ],
        out_specs=pl.BlockSpec((t, heads * hw), lambda b, h, i: (b * nq + i, h)),
        out_shape=jax.ShapeDtypeStruct((m, aw), BF16),
        scratch_shapes=[
            pltpu.VMEM((heads, 2 * t, hw), BF16),
            pltpu.VMEM((heads * (2 * t // qc), t, qc), F32),
            pltpu.VMEM((heads, 1, 2 * t), F32),
            pltpu.VMEM((heads, 1, 2 * t), F32),
            pltpu.VMEM((heads, hw, 2 * t), F32),
        ],
        compiler_params=_params("parallel", "parallel", "arbitrary"),
        name="attn_prompt",
    )(qb, kb, vt, *lam_p, subln_g)


N_DECODE_IN = 8


def _attn_sample_kernel(pt_ref, *refs, n_pages, n_heads):
    p_idx = pl.program_id(1)
    _decode_pages(p_idx == 0, p_idx == pl.num_programs(1) - 1, *refs, n_pages=n_pages, n_heads=n_heads)


def _decode_pages(first, last, q_ref, kn_ref, vn_ref, lq1_ref, lk1_ref, lq2_ref, lk2_ref, sg_ref,
                  *refs, n_pages, n_heads):
    k_refs = refs[:n_pages]
    v_refs = refs[n_pages:2 * n_pages]
    o_ref = refs[2 * n_pages]
    qm_ref, new_ref, s_ref, m_ref, l_ref, acc_ref = refs[2 * n_pages + 1:]
    hw = 2 * HEAD_DIM
    rows = 2 * n_heads
    page_rows = k_refs[0].shape[0]
    lane = lax.broadcasted_iota(jnp.int32, (1, hw), 1)

    @pl.when(first)
    def _():
        q = q_ref[...].astype(F32)
        for h in range(n_heads):
            qh = q[:, h * hw:(h + 1) * hw]
            qm_ref[h:h + 1, :] = jnp.where(lane < HEAD_DIM, qh, 0.0)
            qm_ref[n_heads + h:n_heads + h + 1, :] = jnp.where(lane >= HEAD_DIM, qh, 0.0)
        m_ref[...] = jnp.full(m_ref.shape, NEG_INF, F32)
        l_ref[...] = jnp.zeros(l_ref.shape, F32)
        acc_ref[...] = jnp.zeros(acc_ref.shape, F32)

    qm = qm_ref[...].astype(BF16)
    own = (lax.broadcasted_iota(jnp.int32, (rows, page_rows), 0) % n_heads
           == lax.broadcasted_iota(jnp.int32, (rows, page_rows), 1) % n_heads)
    for pg in range(n_pages):
        s_pg = _dot_nt(qm, k_refs[pg][...].astype(BF16))
        s_ref[:, pg * page_rows:(pg + 1) * page_rows] = jnp.where(own, s_pg, NEG_INF)
    s = s_ref[...]
    m_old = m_ref[...]
    m_new = jnp.maximum(m_old, jnp.max(s, axis=-1, keepdims=True))
    alpha = jnp.exp2(m_old - m_new)
    p = jnp.exp2(s - m_new)
    l_ref[...] = alpha * l_ref[...] + jnp.sum(p, axis=-1, keepdims=True)
    m_ref[...] = m_new
    pv = None
    for pg in range(n_pages):
        d = _dot(p[:, pg * page_rows:(pg + 1) * page_rows].astype(BF16), v_refs[pg][...].astype(BF16))
        pv = d if pv is None else pv + d
    acc_ref[...] = alpha * acc_ref[...] + pv

    @pl.when(last)
    def _():
        lam = _diff_lambda(lq1_ref[...], lk1_ref[...], lq2_ref[...], lk2_ref[...])
        kn = kn_ref[...].astype(BF16).astype(F32)
        vn = vn_ref[...].astype(BF16).astype(F32)
        for c in range(2):
            new_ref[0, c * n_heads:(c + 1) * n_heads, :] = kn
            new_ref[1, c * n_heads:(c + 1) * n_heads, :] = vn
        s_new = jnp.sum(qm_ref[...] * new_ref[0], axis=-1, keepdims=True)
        m_old = m_ref[...]
        m_new = jnp.maximum(m_old, s_new)
        alpha = jnp.exp2(m_old - m_new)
        p_new = jnp.exp2(s_new - m_new)
        l = alpha * l_ref[...] + p_new
        acc = alpha * acc_ref[...] + p_new.astype(BF16).astype(F32) * new_ref[1]
        o = acc / l
        o = _sub_norm(o[0:n_heads, :] - lam * o[n_heads:rows, :], sg_ref[...])
        for h in range(n_heads):
            o_ref[:, h * hw:(h + 1) * hw] = o[h:h + 1, :].astype(o_ref.dtype)


def _attn_sample(qb, k_new, v_new, lam_p, subln_g, cache_k, cache_v, page_table, n_heads):
    n, aw = qb.shape
    hw = aw // n_heads
    page = cache_k.shape[1]
    n_log = page_table.shape[1]
    pp = DEC_PAGES
    while n_log % pp:
        pp //= 2
    row3 = lambda a: a.reshape(n, 1, aw)
    row_spec = pl.BlockSpec((None, 1, aw), lambda b, p, pt: (b, 0, 0))
    new_spec = pl.BlockSpec((None, n_heads, hw), lambda b, p, pt: (b, 0, 0))
    vec = pl.BlockSpec((1, HEAD_DIM), lambda b, p, pt: (0, 0))

    def page_spec(j):
        return pl.BlockSpec((None, page * n_heads, hw), lambda b, p, pt: (pt[b, p * pp + j], 0, 0))

    grid_spec = pltpu.PrefetchScalarGridSpec(
        num_scalar_prefetch=1,
        grid=(n, n_log // pp),
        in_specs=[row_spec, new_spec, new_spec, vec, vec, vec, vec,
                  pl.BlockSpec((1, hw), lambda b, p, pt: (0, 0))]
                 + [page_spec(j) for j in range(pp)] + [page_spec(j) for j in range(pp)],
        out_specs=row_spec,
        scratch_shapes=[
            pltpu.VMEM((2 * n_heads, hw), F32),
            pltpu.VMEM((2, 2 * n_heads, hw), F32),
            pltpu.VMEM((2 * n_heads, pp * page * n_heads), F32),
            pltpu.VMEM((2 * n_heads, 1), F32),
            pltpu.VMEM((2 * n_heads, 1), F32),
            pltpu.VMEM((2 * n_heads, hw), F32),
        ],
    )
    ck = cache_k.reshape(cache_k.shape[0], page * n_heads, hw)
    cv = cache_v.reshape(cache_v.shape[0], page * n_heads, hw)
    out = pl.pallas_call(
        functools.partial(_attn_sample_kernel, n_pages=pp, n_heads=n_heads),
        grid_spec=grid_spec,
        out_shape=jax.ShapeDtypeStruct((n, 1, aw), BF16),
        compiler_params=_params("parallel", "arbitrary"),
        name="attn_sample",
    )(page_table, row3(qb), k_new.reshape(n, n_heads, hw), v_new.reshape(n, n_heads, hw),
      *lam_p, subln_g, *([ck] * pp), *([cv] * pp))
    return out.reshape(n, aw)


def _mixer_decode_kernel(pt_ref, *refs, tm, pad, ka, kb, n_pages, n_heads, groups):
    n_dec_in = N_DECODE_IN + 2 * n_pages
    mixer_in = refs[:N_MIXER_IN]
    dec_in = refs[N_MIXER_IN:N_MIXER_IN + n_dec_in]
    outs = refs[N_MIXER_IN + n_dec_in:N_MIXER_IN + n_dec_in + 5]
    scratch = refs[N_MIXER_IN + n_dec_in + 5:]
    step = pl.program_id(0) * pl.num_programs(1) + pl.program_id(1)
    group = step % groups
    _decode_pages(group == 0, group == groups - 1, *dec_in, outs[4], *scratch[6:],
                  n_pages=n_pages, n_heads=n_heads)
    _mixer_tile(pl.program_id(1) == 0, *mixer_in, *outs[:4], *scratch[:6],
                tm=tm, pad=pad, ka=ka, kb=kb)


def _fused_page_group(bsz, t, n_seq, n_log):
    tm = min(FUSED_MIX_ROWS, t)
    if t % tm:
        return None
    steps = bsz * (t // tm)
    if steps % n_seq or n_log % (steps // n_seq):
        return None
    pp = n_log // (steps // n_seq)
    return pp if pp <= FUSED_MAX_PAGES else None


def _mixer_decode(x, h0, ca0, cb0, mp, g, b, qb, k_new, v_new, lam_p, subln_g,
                  cache_k, cache_v, page_table, n_heads):
    bsz, t, d = x.shape
    d_a = h0.shape[-1]
    ka = ca0.shape[1] + 1
    kb = cb0.shape[1] + 1
    assert ka - 1 <= HIST_A and kb - 1 <= HIST_B
    n, aw = qb.shape
    hw = aw // n_heads
    page = cache_k.shape[1]
    n_log = page_table.shape[1]
    pp = _fused_page_group(bsz, t, n, n_log)
    tm = min(FUSED_MIX_ROWS, t)
    nt = t // tm
    groups = n_log // pp
    pad = max(SUBLANES, tm // 2)

    seq = lambda bi, ti: (bi * nt + ti) // groups
    bt = lambda bi, ti, pt: (bi, ti, 0)
    b0 = lambda bi, ti, pt: (bi, 0, 0)
    s0 = lambda bi, ti, pt: (seq(bi, ti), 0, 0)

    def page_spec(j):
        return pl.BlockSpec(
            (None, page * n_heads, hw),
            lambda bi, ti, pt: (pt[seq(bi, ti), ((bi * nt + ti) % groups) * pp + j], 0, 0))

    vec = _const_spec((1, HEAD_DIM))
    in_specs = [
        pl.BlockSpec((None, tm, d), bt),
        pl.BlockSpec((None, 1, d_a), b0),
        pl.BlockSpec((None, ka - 1, d_a), b0),
        pl.BlockSpec((None, kb - 1, d_a), b0),
        _resident_spec(mp["w_in"].shape),
        _const_spec((ka, d_a)), _const_spec((1, d_a)),
        _resident_spec((d_a, d_a)), _const_spec((1, d_a)),
        _resident_spec((d_a, d_a)), _const_spec((1, d_a)),
        _const_spec((1, d_a)),
        _const_spec((kb, d_a)), _const_spec((1, d_a)),
        _const_spec((1, d_a)), _const_spec((1, d_a)),
        _resident_spec(mp["w_out"].shape),
        _const_spec((1, d)), _const_spec((1, d)),
        pl.BlockSpec((None, 1, aw), s0),
        pl.BlockSpec((None, n_heads, hw), s0),
        pl.BlockSpec((None, n_heads, hw), s0),
        vec, vec, vec, vec, _const_spec((1, hw)),
    ] + [page_spec(j) for j in range(pp)] + [page_spec(j) for j in range(pp)]
    assert len(in_specs) == N_MIXER_IN + N_DECODE_IN + 2 * pp
    grid_spec = pltpu.PrefetchScalarGridSpec(
        num_scalar_prefetch=1,
        grid=(bsz, nt),
        in_specs=in_specs,
        out_specs=[
            pl.BlockSpec((None, tm, d), bt),
            pl.BlockSpec((None, 1, d_a), b0),
            pl.BlockSpec((None, ka - 1, d_a), b0),
            pl.BlockSpec((None, kb - 1, d_a), b0),
            pl.BlockSpec((None, 1, aw), s0),
        ],
        scratch_shapes=[
            pltpu.VMEM((HIST_A + tm, d_a), F32),
            pltpu.VMEM((HIST_B + tm, d_a), F32),
            pltpu.VMEM((SUBLANES - 1, HIST_B - SUBLANES + tm, d_a), F32),
            pltpu.VMEM((pad + tm, d_a), F32),
            pltpu.VMEM((pad + tm, d_a), F32),
            pltpu.VMEM((1, d_a), F32),
            pltpu.VMEM((2 * n_heads, hw), F32),
            pltpu.VMEM((2, 2 * n_heads, hw), F32),
            pltpu.VMEM((2 * n_heads, pp * page * n_heads), F32),
            pltpu.VMEM((2 * n_heads, 1), F32),
            pltpu.VMEM((2 * n_heads, 1), F32),
            pltpu.VMEM((2 * n_heads, hw), F32),
        ],
    )
    ck = cache_k.reshape(cache_k.shape[0], page * n_heads, hw)
    cv = cache_v.reshape(cache_v.shape[0], page * n_heads, hw)
    y, hl, cas, cbs, att = pl.pallas_call(
        functools.partial(_mixer_decode_kernel, tm=tm, pad=pad, ka=ka, kb=kb,
                          n_pages=pp, n_heads=n_heads, groups=groups),
        grid_spec=grid_spec,
        out_shape=[
            jax.ShapeDtypeStruct((bsz, t, d), F32),
            jax.ShapeDtypeStruct((bsz, 1, d_a), F32),
            jax.ShapeDtypeStruct((bsz, ka - 1, d_a), F32),
            jax.ShapeDtypeStruct((bsz, kb - 1, d_a), F32),
            jax.ShapeDtypeStruct((n, 1, aw), BF16),
        ],
        compiler_params=_params("arbitrary", "arbitrary"),
        name="mixer_decode",
    )(page_table, x, h0.reshape(bsz, 1, d_a), ca0, cb0, mp["w_in"], mp["conv_a_w"], mp["conv_a_b"],
      mp["wa"], mp["ba"], mp["wx"], mp["bx"], mp["lam"], mp["conv_b_w"], mp["conv_b_b"],
      mp["cln_g"], mp["cln_b"], mp["w_out"], g, b,
      qb.reshape(n, 1, aw), k_new.reshape(n, n_heads, hw), v_new.reshape(n, n_heads, hw),
      *lam_p, subln_g, *([ck] * pp), *([cv] * pp))
    return y, hl.reshape(bsz, d_a), cas, cbs, att.reshape(n, aw)


def _proj_ln_kernel(a_ref, x_ref, w_ref, g_ref, b_ref, o_ref):
    y = _dot(a_ref[...], w_ref[...])
    o_ref[...] = _layer_norm(ALPHA * x_ref[...] + y, g_ref[...], b_ref[...])


def _proj_ln(a, x, w, g, b):
    m, d = x.shape
    tm = min(PROJ_ROWS, m)
    return pl.pallas_call(
        _proj_ln_kernel,
        grid=(pl.cdiv(m, tm),),
        in_specs=[pl.BlockSpec((tm, a.shape[1]), lambda i: (i, 0)),
                  pl.BlockSpec((tm, d), lambda i: (i, 0)),
                  _const_spec(w.shape), _const_spec((1, d)), _const_spec((1, d))],
        out_specs=pl.BlockSpec((tm, d), lambda i: (i, 0)),
        out_shape=jax.ShapeDtypeStruct((m, d), F32),
        compiler_params=_params("parallel"),
        name="proj_ln",
    )(a, x, w, g, b)


def _block_diag(w):
    n, bi, bj = w.shape
    eye = jnp.eye(n, dtype=w.dtype)
    return jnp.einsum("nij,nm->nimj", w, eye).reshape(n * bi, n * bj)


def kernel(x_prompt, x_sample, state_lru_h, state_conv_a, state_conv_b, cache_k, cache_v, page_table,
           ln_g, ln_b, ffn_w_gate, ffn_w_up, ffn_w_down,
           rec_w_in, conv_a_w, conv_a_b, lru_w_a, lru_b_a, lru_w_x, lru_b_x, lru_lambda,
           conv_b_w, conv_b_b, conv_ln_g, conv_ln_b, rec_w_out,
           att_w_qkv, lambda_q1, lambda_k1, lambda_q2, lambda_k2, subln_g, att_w_out):
    bp, tp, d = x_prompt.shape
    bs, ts, _ = x_sample.shape
    assert ts == 1, "the sample group decodes one token per sequence"
    n_heads = cache_k.shape[2]
    d_a = state_lru_h.shape[-1]
    d_ff = ffn_w_gate.shape[-1]
    assert d_ff % FFN_CHUNK == 0
    row = lambda v: v.reshape(1, -1)

    ffn_w = (ffn_w_gate.astype(BF16), ffn_w_up.astype(BF16), ffn_w_down.astype(BF16))
    lng = [[row(ln_g[l, i]) for i in range(3)] for l in range(DEPTH)]
    lnb = [[row(ln_b[l, i]) for i in range(3)] for l in range(DEPTH)]
    mp = dict(
        w_in=rec_w_in.astype(BF16), conv_a_w=conv_a_w, conv_a_b=row(conv_a_b),
        wa=_block_diag(lru_w_a).astype(BF16), ba=row(lru_b_a),
        wx=_block_diag(lru_w_x).astype(BF16), bx=row(lru_b_x), lam=row(lru_lambda),
        conv_b_w=conv_b_w, conv_b_b=row(conv_b_b), cln_g=row(conv_ln_g), cln_b=row(conv_ln_b),
        w_out=rec_w_out.astype(BF16))
    w_qkv = att_w_qkv.astype(BF16)
    w_o = att_w_out.astype(BF16)
    lam_p = (row(lambda_q1), row(lambda_k1), row(lambda_q2), row(lambda_k2))
    sg = row(subln_g)

    def ffn_sub(x, l, i, proj=None):
        return _ffn_sublayer(x, *ffn_w, l, i, lng[l][2 * i], lnb[l][2 * i], proj)

    past = page_table.shape[1] * cache_k.shape[1]
    xs = x_sample.reshape(bs, d)
    xs = ffn_sub(xs, 0, 0)
    xs, h_s, ca_s, cb_s = _mixer_sample(xs, state_lru_h, state_conv_a, state_conv_b, mp,
                                        lng[0][1], lnb[0][1])
    xs = ffn_sub(xs, 0, 1)
    xs = ffn_sub(xs, 1, 0)
    cos_s, sin_s = _rope_tables(past + jnp.arange(ts, dtype=jnp.int32))
    k_s, v_s, qsb = _qkv_rope(xs, w_qkv, cos_s, sin_s, n_heads, ts)

    x = x_prompt.reshape(bp * tp, d)
    x = ffn_sub(x, 0, 0).reshape(bp, tp, d)
    zero_state = (jnp.zeros((bp, d_a), F32), jnp.zeros((bp, state_conv_a.shape[1], d_a), F32),
                  jnp.zeros((bp, state_conv_b.shape[1], d_a), F32))
    if _fused_page_group(bp, tp, bs, page_table.shape[1]) is not None:
        x, h_p, ca_p, cb_p, att_s = _mixer_decode(
            x, *zero_state, mp, lng[0][1], lnb[0][1], qsb, k_s, v_s, lam_p, sg,
            cache_k, cache_v, page_table, n_heads)
    else:
        x, h_p, ca_p, cb_p = _mixer_prompt(x, *zero_state, mp, lng[0][1], lnb[0][1])
        att_s = _attn_sample(qsb, k_s, v_s, lam_p, sg, cache_k, cache_v, page_table, n_heads)
    x = ffn_sub(x.reshape(bp * tp, d), 0, 1)
    x = ffn_sub(x, 1, 0)
    cos_p, sin_p = _rope_tables(jnp.arange(tp, dtype=jnp.int32))
    att_tile = min(ATT_TILE, tp)
    k_p, v_p, qb, kb, vt = _qkv_rope(x, w_qkv, cos_p, sin_p, n_heads, tp, att_tile)
    att = _attn_prompt(qb, kb, vt, lam_p, sg, bp, tp, n_heads, att_tile)
    y_prompt = ffn_sub(x, 1, 1, (att, w_o, lng[1][1], lnb[1][1])).reshape(bp, tp, d)

    y_sample = ffn_sub(xs, 1, 1, (att_s, w_o, lng[1][1], lnb[1][1])).reshape(bs, ts, d)

    hw = 2 * HEAD_DIM
    return (y_prompt, y_sample, h_p, ca_p, cb_p,
            k_p.reshape(bp, tp, n_heads, hw), v_p.reshape(bp, tp, n_heads, hw),
            h_s, ca_s, cb_s,
            k_s.reshape(bs, ts, n_heads, hw), v_s.reshape(bs, ts, n_heads, hw))
```

```python
import functools
import math

import jax
import jax.numpy as jnp
from jax import lax
from jax.experimental import pallas as pl
from jax.experimental.pallas import tpu as pltpu

F32 = jnp.float32
BF16 = jnp.bfloat16

DEPTH = 2
LRU_BLOCKS = 8
LRU_C = 8.0
HEAD_DIM = 64
ROPE_THETA = 10000.0
ATT_SCALE = HEAD_DIM ** -0.5
LOG2_E = math.log2(math.e)
NEG_INF = -1e30
ALPHA = (2 * DEPTH) ** 0.25
LN_EPS = 1e-5
LAMBDA_INIT = 0.8 - 0.6 * math.exp(-0.3 * 1)

LANES = 128
SUBLANES = 8
VMEM_LIMIT_BYTES = 56 * 1024 * 1024

FFN_CHUNK = 256
FFN_ROWS = 512
FFN_STREAM_MAX_ROWS = 64
MIX_ROWS = 256
FUSED_MIX_ROWS = 128
FUSED_MAX_PAGES = 16
QKV_ROWS = 512
ATT_TILE = 512
ATT_HEADS = 4
ATT_QCHUNK = 512
ATT_LOOKAHEAD = 1
PROJ_ROWS = 512
DEC_PAGES = 8


def _layer_norm(y, g, b):
    mu = jnp.mean(y, axis=-1, keepdims=True)
    d = y - mu
    var = jnp.mean(d * d, axis=-1, keepdims=True)
    return d * lax.rsqrt(var + LN_EPS) * g + b


def _sigmoid(x):
    return 1.0 / (1.0 + jnp.exp(-x))


def _dot(a, b):
    return jnp.dot(a, b, preferred_element_type=F32)


def _dot_nt(a, b):
    return lax.dot_general(a, b, (((1,), (1,)), ((), ())), preferred_element_type=F32)


def _const_spec(shape):
    nd = len(shape)
    return pl.BlockSpec(shape, lambda *_: (0,) * nd)


def _resident_spec(shape):
    nd = len(shape)
    return pl.BlockSpec(shape, lambda *_: (0,) * nd, pipeline_mode=pl.Buffered(1))


def _params(*sem):
    return pltpu.CompilerParams(dimension_semantics=sem, vmem_limit_bytes=VMEM_LIMIT_BYTES)


def _swiglu(xb, wg_ref, wu_ref, wd_ref):
    d_ff = wg_ref.shape[1]
    acc = None
    for c in range(d_ff // FFN_CHUNK):
        cols = slice(c * FFN_CHUNK, (c + 1) * FFN_CHUNK)
        g = _dot(xb, wg_ref[:, cols])
        u = _dot(xb, wu_ref[:, cols])
        d = _dot((g * _sigmoid(g) * u).astype(BF16), wd_ref[cols, :])
        acc = d if acc is None else acc + d
    return acc


def _ffn_kernel(*refs, with_proj):
    if with_proj:
        a_ref, xr_ref, wp_ref, gp_ref, bp_ref, wg_ref, wu_ref, wd_ref, g_ref, b_ref, o_ref = refs
        x = _layer_norm(ALPHA * xr_ref[...] + _dot(a_ref[...], wp_ref[...]), gp_ref[...], bp_ref[...])
    else:
        x_ref, wg_ref, wu_ref, wd_ref, g_ref, b_ref, o_ref = refs
        x = x_ref[...]
    acc = _swiglu(x.astype(BF16), wg_ref, wu_ref, wd_ref)
    o_ref[...] = _layer_norm(ALPHA * x + 0.5 * acc, g_ref[...], b_ref[...])


def _ffn_stream_kernel(x_ref, wg_ref, wu_ref, wd_ref, g_ref, b_ref, o_ref, acc_ref):
    c = pl.program_id(0)
    x = x_ref[...]
    xb = x.astype(BF16)
    g = _dot(xb, wg_ref[...])
    u = _dot(xb, wu_ref[...])
    d = _dot((g * _sigmoid(g) * u).astype(BF16), wd_ref[...])

    @pl.when(c == 0)
    def _():
        acc_ref[...] = d

    @pl.when(c > 0)
    def _():
        acc_ref[...] += d

    @pl.when(c == pl.num_programs(0) - 1)
    def _():
        o_ref[...] = _layer_norm(ALPHA * x + 0.5 * acc_ref[...], g_ref[...], b_ref[...])


def _ffn_sublayer(x, wg, wu, wd, layer, idx, g, b, proj=None):
    m, d = x.shape
    d_ff = wg.shape[-1]
    fc = FFN_CHUNK
    assert d_ff % fc == 0
    if m <= FFN_STREAM_MAX_ROWS:
        if proj is not None:
            x = _proj_ln(proj[0], x, *proj[1:])
        return pl.pallas_call(
            _ffn_stream_kernel,
            grid=(d_ff // fc,),
            in_specs=[
                _const_spec((m, d)),
                pl.BlockSpec((None, None, d, fc), lambda c: (layer, idx, 0, c)),
                pl.BlockSpec((None, None, d, fc), lambda c: (layer, idx, 0, c)),
                pl.BlockSpec((None, None, fc, d), lambda c: (layer, idx, c, 0)),
                _const_spec((1, d)),
                _const_spec((1, d)),
            ],
            out_specs=_const_spec((m, d)),
            out_shape=jax.ShapeDtypeStruct((m, d), F32),
            scratch_shapes=[pltpu.VMEM((m, d), F32)],
            compiler_params=_params("arbitrary"),
            name="ffn_stream",
        )(x, wg, wu, wd, g, b)
    tm = min(FFN_ROWS, m)
    rows = pl.BlockSpec((tm, d), lambda i: (i, 0))
    if proj is None:
        lead_args, lead_specs = (x,), [rows]
    else:
        a, wp, gp, bp = proj
        lead_args = (a, x, wp, gp, bp)
        lead_specs = [pl.BlockSpec((tm, a.shape[1]), lambda i: (i, 0)), rows,
                      _resident_spec(wp.shape), _const_spec((1, d)), _const_spec((1, d))]
    pick = lambda *_: (layer, idx, 0, 0)
    resident = pl.Buffered(1)
    return pl.pallas_call(
        functools.partial(_ffn_kernel, with_proj=proj is not None),
        grid=(pl.cdiv(m, tm),),
        in_specs=lead_specs + [
            pl.BlockSpec((None, None, d, d_ff), pick, pipeline_mode=resident),
            pl.BlockSpec((None, None, d, d_ff), pick, pipeline_mode=resident),
            pl.BlockSpec((None, None, d_ff, d), pick, pipeline_mode=resident),
            _const_spec((1, d)),
            _const_spec((1, d)),
        ],
        out_specs=rows,
        out_shape=jax.ShapeDtypeStruct((m, d), F32),
        compiler_params=_params("parallel"),
        name="ffn_sublayer",
    )(*lead_args, wg, wu, wd, g, b)


def _softplus(x):
    return jnp.maximum(x, 0.0) + jnp.log1p(jnp.exp(-jnp.abs(x)))


def _gelu_tanh(x):
    c = math.sqrt(2.0 / math.pi)
    return 0.5 * x * (1.0 + jnp.tanh(c * (x + 0.044715 * (x * x * x))))


def _lru_gates(xa_c, wa, ba, wx, bx, lam):
    xb = xa_c.astype(BF16)
    r = _sigmoid(_dot(xb, wa) + ba)
    i = _sigmoid(_dot(xb, wx) + bx)
    log_a = (-LRU_C) * r * _softplus(-lam)
    a = jnp.exp(log_a)
    th = jnp.tanh(log_a)
    y = -2.0 * th / (1.0 - th)
    u = jnp.where(y > 0.0, y * lax.rsqrt(y), 0.0) * (i * xa_c)
    return a, u


def _conv_b_post(acc, g, b):
    y = _layer_norm(acc, g, b)
    return y * _sigmoid(y)


HIST_A = 8
HIST_B = 32


N_MIXER_IN = 19


def _trace_interleaved(*bodies):
    live = list(bodies)
    while live:
        for body in list(live):
            if next(body, StopIteration) is StopIteration:
                live.remove(body)


def _mixer_prompt_kernel(*refs, tm, pad, ka, kb):
    _trace_interleaved(_mixer_tile(pl.program_id(1) == 0, *refs, tm=tm, pad=pad, ka=ka, kb=kb))


def _mixer_tile(first, x_ref, h0_ref, ca0_ref, cb0_ref, w_in_ref, caw_ref, cab_ref,
                wa_ref, ba_ref, wx_ref, bx_ref, lam_ref, cbw_ref, cbb_ref,
                clg_ref, clb_ref, wout_ref, g_ref, b_ref,
                y_ref, hl_ref, cas_ref, cbs_ref,
                ha_ref, hb_ref, sh_ref, sa_ref, su_ref, hc_ref, *, tm, pad, ka, kb):
    d_a = ha_ref.shape[1]

    @pl.when(first)
    def _():
        ha_ref[HIST_A - (ka - 1):HIST_A, :] = ca0_ref[...]
        hb_ref[HIST_B - (kb - 1):HIST_B, :] = cb0_ref[...]
        hc_ref[...] = h0_ref[...]
        sa_ref[0:pad, :] = jnp.ones((pad, d_a), F32)
        su_ref[0:pad, :] = jnp.zeros((pad, d_a), F32)

    x = x_ref[...]
    z = _dot(x.astype(BF16), w_in_ref[...])
    gate_a = z[:, 0:d_a]
    xa = z[:, d_a:2 * d_a]
    vb = z[:, 2 * d_a:3 * d_a]
    gb = z[:, 3 * d_a:4 * d_a]

    ha_ref[HIST_A:HIST_A + tm, :] = xa
    xa_c = cab_ref[...]
    for j in range(ka):
        s = HIST_A - (ka - 1) + j
        xa_c = xa_c + caw_ref[j:j + 1, :] * ha_ref[s:s + tm, :]
    a, u = _lru_gates(xa_c, wa_ref[...], ba_ref[...], wx_ref[...], bx_ref[...], lam_ref[...])
    sa_ref[pad:pad + tm, :] = a
    su_ref[pad:pad + tm, :] = u
    su_ref[pad:pad + 1, :] = u[0:1, :] + a[0:1, :] * hc_ref[...]
    yield
    k = 1
    while k < tm:
        a_cur = sa_ref[pad:pad + tm, :]
        u_cur = su_ref[pad:pad + tm, :]
        a_sh = sa_ref[pad - k:pad - k + tm, :]
        u_sh = su_ref[pad - k:pad - k + tm, :]
        su_ref[pad:pad + tm, :] = u_cur + a_cur * u_sh
        if 2 * k < tm:
            sa_ref[pad:pad + tm, :] = a_cur * a_sh
        k *= 2
        yield
    h_seq = su_ref[pad:pad + tm, :]
    h_last = h_seq[tm - 1:tm, :]
    hc_ref[...] = h_last
    hl_ref[...] = h_last
    ya = h_seq * _gelu_tanh(gate_a)
    yield

    hb_ref[HIST_B:HIST_B + tm, :] = vb * _sigmoid(gb)
    span = HIST_B - SUBLANES + tm
    acc = cbb_ref[...]
    for r in range(SUBLANES):
        taps = [j for j in range(kb) if (HIST_B - (kb - 1) + j) % SUBLANES == r]
        if not taps:
            continue
        if r:
            sh_ref[r - 1] = hb_ref[r:r + span, :]
        for j in taps:
            s = HIST_B - (kb - 1) + j - r
            rows = sh_ref[r - 1, s:s + tm, :] if r else hb_ref[s:s + tm, :]
            acc = acc + cbw_ref[j:j + 1, :] * rows
        yield
    yb = _conv_b_post(acc, clg_ref[...], clb_ref[...])
    yield

    m = _dot(ya.astype(BF16), wout_ref[0:d_a, :]) + _dot(yb.astype(BF16), wout_ref[d_a:2 * d_a, :])
    y_ref[...] = _layer_norm(ALPHA * x + m, g_ref[...], b_ref[...])

    tail_a = ha_ref[tm + HIST_A - (ka - 1):tm + HIST_A, :]
    cas_ref[...] = tail_a
    ha_ref[HIST_A - (ka - 1):HIST_A, :] = tail_a
    tail_b = hb_ref[tm + HIST_B - (kb - 1):tm + HIST_B, :]
    cbs_ref[...] = tail_b
    hb_ref[HIST_B - (kb - 1):HIST_B, :] = tail_b


def _mixer_prompt(x, h0, ca0, cb0, mp, g, b):
    bsz, t, d = x.shape
    d_a = h0.shape[-1]
    ka = ca0.shape[1] + 1
    kb = cb0.shape[1] + 1
    assert ka - 1 <= HIST_A and kb - 1 <= HIST_B
    tm = min(MIX_ROWS, t)
    assert t % tm == 0
    pad = max(SUBLANES, tm // 2)
    kernel = functools.partial(_mixer_prompt_kernel, tm=tm, pad=pad, ka=ka, kb=kb)
    bt = lambda bi, ti: (bi, ti, 0)
    b0 = lambda bi, ti: (bi, 0, 0)
    outs = pl.pallas_call(
        kernel,
        grid=(bsz, t // tm),
        in_specs=[
            pl.BlockSpec((None, tm, d), bt),
            pl.BlockSpec((None, 1, d_a), b0),
            pl.BlockSpec((None, ka - 1, d_a), b0),
            pl.BlockSpec((None, kb - 1, d_a), b0),
            _const_spec(mp["w_in"].shape),
            _const_spec((ka, d_a)), _const_spec((1, d_a)),
            _const_spec((d_a, d_a)), _const_spec((1, d_a)),
            _const_spec((d_a, d_a)), _const_spec((1, d_a)),
            _const_spec((1, d_a)),
            _const_spec((kb, d_a)), _const_spec((1, d_a)),
            _const_spec((1, d_a)), _const_spec((1, d_a)),
            _const_spec(mp["w_out"].shape),
            _const_spec((1, d)), _const_spec((1, d)),
        ],
        out_specs=[
            pl.BlockSpec((None, tm, d), bt),
            pl.BlockSpec((None, 1, d_a), b0),
            pl.BlockSpec((None, ka - 1, d_a), b0),
            pl.BlockSpec((None, kb - 1, d_a), b0),
        ],
        out_shape=[
            jax.ShapeDtypeStruct((bsz, t, d), F32),
            jax.ShapeDtypeStruct((bsz, 1, d_a), F32),
            jax.ShapeDtypeStruct((bsz, ka - 1, d_a), F32),
            jax.ShapeDtypeStruct((bsz, kb - 1, d_a), F32),
        ],
        scratch_shapes=[
            pltpu.VMEM((HIST_A + tm, d_a), F32),
            pltpu.VMEM((HIST_B + tm, d_a), F32),
            pltpu.VMEM((SUBLANES - 1, HIST_B - SUBLANES + tm, d_a), F32),
            pltpu.VMEM((pad + tm, d_a), F32),
            pltpu.VMEM((pad + tm, d_a), F32),
            pltpu.VMEM((1, d_a), F32),
        ],
        compiler_params=_params("parallel", "arbitrary"),
        name="mixer_prompt",
    )(x, h0.reshape(bsz, 1, d_a), ca0, cb0, mp["w_in"], mp["conv_a_w"], mp["conv_a_b"],
      mp["wa"], mp["ba"], mp["wx"], mp["bx"], mp["lam"], mp["conv_b_w"], mp["conv_b_b"],
      mp["cln_g"], mp["cln_b"], mp["w_out"], g, b)
    y, hl, cas, cbs = outs
    return y, hl.reshape(bsz, d_a), cas, cbs


def _mixer_sample_kernel(x_ref, h0_ref, ca_ref, cb_ref, w_in_ref, caw_ref, cab_ref,
                         wa_ref, ba_ref, wx_ref, bx_ref, lam_ref, cbw_ref, cbb_ref,
                         clg_ref, clb_ref, wout_ref, g_ref, b_ref,
                         y_ref, hl_ref, cas_ref, cbs_ref, *, ka, kb):
    d_a = h0_ref.shape[1]
    x = x_ref[...]
    z = _dot(x.astype(BF16), w_in_ref[...])
    gate_a = z[:, 0:d_a]
    xa = z[:, d_a:2 * d_a]
    vb = z[:, 2 * d_a:3 * d_a]
    gb = z[:, 3 * d_a:4 * d_a]

    xa_c = cab_ref[...]
    for j in range(ka - 1):
        xa_c = xa_c + caw_ref[j:j + 1, :] * ca_ref[j]
    xa_c = xa_c + caw_ref[ka - 1:ka, :] * xa
    a, u = _lru_gates(xa_c, wa_ref[...], ba_ref[...], wx_ref[...], bx_ref[...], lam_ref[...])
    h = a * h0_ref[...] + u
    hl_ref[...] = h
    ya = h * _gelu_tanh(gate_a)

    ub = vb * _sigmoid(gb)
    acc = cbb_ref[...]
    for j in range(kb - 1):
        acc = acc + cbw_ref[j:j + 1, :] * cb_ref[j]
    acc = acc + cbw_ref[kb - 1:kb, :] * ub
    yb = _conv_b_post(acc, clg_ref[...], clb_ref[...])

    m = _dot(ya.astype(BF16), wout_ref[0:d_a, :]) + _dot(yb.astype(BF16), wout_ref[d_a:2 * d_a, :])
    y_ref[...] = _layer_norm(ALPHA * x + m, g_ref[...], b_ref[...])

    for j in range(ka - 2):
        cas_ref[j] = ca_ref[j + 1]
    cas_ref[ka - 2] = xa
    for j in range(kb - 2):
        cbs_ref[j] = cb_ref[j + 1]
    cbs_ref[kb - 2] = ub


def _mixer_sample(x, h0, ca0, cb0, mp, g, b):
    n, d = x.shape
    d_a = h0.shape[-1]
    ka = ca0.shape[1] + 1
    kb = cb0.shape[1] + 1
    ca_t = jnp.swapaxes(ca0, 0, 1)
    cb_t = jnp.swapaxes(cb0, 0, 1)
    args = (x, h0, ca_t, cb_t, mp["w_in"], mp["conv_a_w"], mp["conv_a_b"],
            mp["wa"], mp["ba"], mp["wx"], mp["bx"], mp["lam"], mp["conv_b_w"], mp["conv_b_b"],
            mp["cln_g"], mp["cln_b"], mp["w_out"], g, b)
    y, hl, cas, cbs = pl.pallas_call(
        functools.partial(_mixer_sample_kernel, ka=ka, kb=kb),
        grid=(1,),
        in_specs=[_const_spec(a.shape) for a in args],
        out_specs=[_const_spec((n, d)), _const_spec((n, d_a)),
                   _const_spec((ka - 1, n, d_a)), _const_spec((kb - 1, n, d_a))],
        out_shape=[
            jax.ShapeDtypeStruct((n, d), F32),
            jax.ShapeDtypeStruct((n, d_a), F32),
            jax.ShapeDtypeStruct((ka - 1, n, d_a), F32),
            jax.ShapeDtypeStruct((kb - 1, n, d_a), F32),
        ],
        compiler_params=_params("arbitrary"),
        name="mixer_sample",
    )(*args)
    return y, hl, jnp.swapaxes(cas, 0, 1), jnp.swapaxes(cbs, 0, 1)


def _rope_tables(pos):
    half = HEAD_DIM // 2
    inv = ROPE_THETA ** (-jnp.arange(half, dtype=F32) / half)
    ang = pos.astype(F32)[:, None] * inv[None, :]
    cos = jnp.cos(ang)
    sin = jnp.sin(ang)
    c = jnp.concatenate([cos, cos, cos, cos], axis=-1)
    s = jnp.concatenate([-sin, sin, -sin, sin], axis=-1)
    return c, s


def _qkv_rope_kernel(x_ref, w_ref, cos_ref, sin_ref, k_ref, v_ref, qb_ref, *tile_refs,
                     n_heads, att_tile):
    aw = n_heads * 2 * HEAD_DIM
    tm = x_ref.shape[0]
    z = _dot(x_ref[...].astype(BF16), w_ref[...])
    c = cos_ref[...]
    s = sin_ref[...]
    hw = 2 * HEAD_DIM
    lane = lax.broadcasted_iota(jnp.int32, (1, hw), 1)
    first_half = (lane % HEAD_DIM) < (HEAD_DIM // 2)

    def rot(xh):
        swapped = jnp.where(first_half, pltpu.roll(xh, hw - HEAD_DIM // 2, axis=1),
                            pltpu.roll(xh, HEAD_DIM // 2, axis=1))
        return xh * c + swapped * s

    for h in range(n_heads):
        sl = slice(h * hw, (h + 1) * hw)
        qh = rot(z[:, h * hw:(h + 1) * hw])
        kh = rot(z[:, aw + h * hw:aw + (h + 1) * hw])
        vh = z[:, 2 * aw + h * hw:2 * aw + (h + 1) * hw]
        qb_ref[:, sl] = (qh * (ATT_SCALE * LOG2_E)).astype(BF16)
        k_ref[pl.ds(h, tm, stride=n_heads), :] = kh
        v_ref[pl.ds(h, tm, stride=n_heads), :] = vh
        if att_tile is not None:
            kb_ref, vt_ref = tile_refs
            kb_ref[:, sl] = kh.astype(BF16)
            for ti in range(vt_ref.shape[0]):
                vt_ref[ti, h] = vh[ti * att_tile:(ti + 1) * att_tile, :].T.astype(BF16)


def _qkv_rope(x, w_qkv, cos, sin, n_heads, t_per_seq, att_tile=None):
    m, d = x.shape
    aw = n_heads * 2 * HEAD_DIM
    hw = 2 * HEAD_DIM
    if t_per_seq == 1:
        tm = m
        tab_spec = pl.BlockSpec((1, hw), lambda i: (0, 0))
    else:
        tm = min(QKV_ROWS, t_per_seq)
        assert t_per_seq % tm == 0
        nt = t_per_seq // tm
        tab_spec = pl.BlockSpec((tm, hw), lambda i: (i % nt, 0))
    row = pl.BlockSpec((tm, aw), lambda i: (i, 0))
    cache_rows = pl.BlockSpec((tm * n_heads, hw), lambda i: (i, 0))
    out_specs = [cache_rows, cache_rows, row]
    out_shape = [jax.ShapeDtypeStruct((m * n_heads, hw), F32),
                 jax.ShapeDtypeStruct((m * n_heads, hw), F32),
                 jax.ShapeDtypeStruct((m, aw), BF16)]
    if att_tile is not None:
        assert tm % att_tile == 0
        tiles = tm // att_tile
        out_specs += [row, pl.BlockSpec((tiles, n_heads, hw, att_tile), lambda i: (i, 0, 0, 0))]
        out_shape += [jax.ShapeDtypeStruct((m, aw), BF16),
                      jax.ShapeDtypeStruct((m // att_tile, n_heads, hw, att_tile), BF16)]
    return pl.pallas_call(
        functools.partial(_qkv_rope_kernel, n_heads=n_heads, att_tile=att_tile),
        grid=(m // tm,),
        in_specs=[pl.BlockSpec((tm, d), lambda i: (i, 0)), _const_spec(w_qkv.shape),
                  tab_spec, tab_spec],
        out_specs=out_specs,
        out_shape=out_shape,
        compiler_params=_params("parallel"),
        name="qkv_rope",
    )(x, w_qkv, cos, sin)


def _diff_lambda(lq1, lk1, lq2, lk2):
    s1 = jnp.sum(lq1 * lk1, axis=-1, keepdims=True)
    s2 = jnp.sum(lq2 * lk2, axis=-1, keepdims=True)
    return jnp.exp(s1) - jnp.exp(s2) + LAMBDA_INIT


def _sub_norm(o, g):
    return o * lax.rsqrt(jnp.mean(o * o, axis=-1, keepdims=True) + LN_EPS) * g * (1.0 - LAMBDA_INIT)


def _attn_prompt_kernel(q_ref, k_ref, vt_ref, lq1_ref, lk1_ref, lq2_ref, lk2_ref, sg_ref, o_ref,
                        q2_ref, s_ref, m_ref, l_ref, acc_ref, *, t, heads, qc):
    i = pl.program_id(2)
    hw = 2 * HEAD_DIM
    lane = lax.broadcasted_iota(jnp.int32, (t, hw), 1)
    for g in range(heads):
        q = q_ref[:, g * hw:(g + 1) * hw]
        zero = jnp.zeros_like(q)
        q2_ref[g, 0:t, :] = jnp.where(lane < HEAD_DIM, q, zero)
        q2_ref[g, t:2 * t, :] = jnp.where(lane >= HEAD_DIM, q, zero)
        m_ref[g] = jnp.full((1, 2 * t), NEG_INF, F32)
        l_ref[g] = jnp.zeros((1, 2 * t), F32)
        acc_ref[g] = jnp.zeros((hw, 2 * t), F32)

    chains = [(g, c) for g in range(heads) for c in range(2 * t // qc)]
    ahead = min(ATT_LOOKAHEAD, len(chains))

    def scores(off, g, c):
        kt = k_ref[pl.ds(off, t), g * hw:(g + 1) * hw]
        return _dot_nt(kt, q2_ref[g, c * qc:(c + 1) * qc, :])

    def step(j, masked):
        off = pl.multiple_of(j * t, t)
        for idx, (g, c) in enumerate(chains):
            nxt = idx + ahead
            if nxt < len(chains):
                s_ref[nxt] = scores(off, *chains[nxt])
            elif not masked:
                s_ref[nxt - len(chains)] = scores(pl.multiple_of(off + t, t), *chains[nxt - len(chains)])
            cs = slice(c * qc, (c + 1) * qc)
            s = s_ref[idx]
            if masked:
                key_pos = lax.broadcasted_iota(jnp.int32, (t, qc), 0)
                qry_pos = lax.broadcasted_iota(jnp.int32, (t, qc), 1) + (c * qc) % t
                s = jnp.where(key_pos <= qry_pos, s, NEG_INF)
            m_old = m_ref[g, :, cs]
            m_new = jnp.maximum(m_old, jnp.max(s, axis=0, keepdims=True))
            alpha = jnp.exp2(m_old - m_new)
            p = jnp.exp2(s - m_new)
            l_ref[g, :, cs] = alpha * l_ref[g, :, cs] + jnp.sum(p, axis=0, keepdims=True)
            m_ref[g, :, cs] = m_new
            acc_ref[g, :, cs] = alpha * acc_ref[g, :, cs] + _dot(vt_ref[j, g], p.astype(BF16))

    def body(j, carry):
        step(j, False)
        return carry

    for idx in range(ahead):
        s_ref[idx] = scores(0, *chains[idx])
    lax.fori_loop(0, i, body, 0)
    step(i, True)

    lam = _diff_lambda(lq1_ref[...], lk1_ref[...], lq2_ref[...], lk2_ref[...])
    for g in range(heads):
        a = acc_ref[g] * (1.0 / l_ref[g])
        o = (a[:, 0:t] - lam * a[:, t:2 * t]).T
        o_ref[:, g * hw:(g + 1) * hw] = _sub_norm(o, sg_ref[...]).astype(o_ref.dtype)


def _attn_prompt(qb, kb, vt, lam_p, subln_g, bsz, t_seq, n_heads, t):
    m, aw = qb.shape
    hw = aw // n_heads
    heads = min(ATT_HEADS, n_heads)
    assert t_seq % t == 0 and n_heads % heads == 0
    nq = t_seq // t
    vec = _const_spec((1, HEAD_DIM))
    qc = math.gcd(ATT_QCHUNK, t)
    return pl.pallas_call(
        functools.partial(_attn_prompt_kernel, t=t, heads=heads, qc=qc),
        grid=(bsz, n_heads // heads, nq),
        in_specs=[
            pl.BlockSpec((t, heads * hw), lambda b, h, i: (b * nq + i, h)),
            pl.BlockSpec((t_seq, heads * hw), lambda b, h, i: (b, h)),
            pl.BlockSpec((nq, heads, hw, t), lambda b, h, i: (b, h, 0, 0)),
            vec, vec, vec, vec, _const_spec((1, hw)),
        ],
        out_specs=pl.BlockSpec((t, heads * hw), lambda b, h, i: (b * nq + i, h)),
        out_shape=jax.ShapeDtypeStruct((m, aw), BF16),
        scratch_shapes=[
            pltpu.VMEM((heads, 2 * t, hw), BF16),
            pltpu.VMEM((heads * (2 * t // qc), t, qc), F32),
            pltpu.VMEM((heads, 1, 2 * t), F32),
            pltpu.VMEM((heads, 1, 2 * t), F32),
            pltpu.VMEM((heads, hw, 2 * t), F32),
        ],
        compiler_params=_params("parallel", "parallel", "arbitrary"),
        name="attn_prompt",
    )(qb, kb, vt, *lam_p, subln_g)


N_DECODE_IN = 8


def _attn_sample_kernel(pt_ref, *refs, n_pages, n_heads):
    p_idx = pl.program_id(1)
    _trace_interleaved(_decode_pages(p_idx == 0, p_idx == pl.num_programs(1) - 1, *refs,
                                     n_pages=n_pages, n_heads=n_heads))


def _decode_pages(first, last, q_ref, kn_ref, vn_ref, lq1_ref, lk1_ref, lq2_ref, lk2_ref, sg_ref,
                  *refs, n_pages, n_heads):
    k_refs = refs[:n_pages]
    v_refs = refs[n_pages:2 * n_pages]
    o_ref = refs[2 * n_pages]
    qm_ref, new_ref, s_ref, m_ref, l_ref, acc_ref = refs[2 * n_pages + 1:]
    hw = 2 * HEAD_DIM
    rows = 2 * n_heads
    page_rows = k_refs[0].shape[0]
    lane = lax.broadcasted_iota(jnp.int32, (1, hw), 1)

    @pl.when(first)
    def _():
        q = q_ref[...].astype(F32)
        for h in range(n_heads):
            qh = q[:, h * hw:(h + 1) * hw]
            qm_ref[h:h + 1, :] = jnp.where(lane < HEAD_DIM, qh, 0.0)
            qm_ref[n_heads + h:n_heads + h + 1, :] = jnp.where(lane >= HEAD_DIM, qh, 0.0)
        m_ref[...] = jnp.full(m_ref.shape, NEG_INF, F32)
        l_ref[...] = jnp.zeros(l_ref.shape, F32)
        acc_ref[...] = jnp.zeros(acc_ref.shape, F32)

    qm = qm_ref[...].astype(BF16)
    own = (lax.broadcasted_iota(jnp.int32, (rows, page_rows), 0) % n_heads
           == lax.broadcasted_iota(jnp.int32, (rows, page_rows), 1) % n_heads)
    for pg in range(n_pages):
        s_pg = _dot_nt(qm, k_refs[pg][...].astype(BF16))
        s_ref[:, pg * page_rows:(pg + 1) * page_rows] = jnp.where(own, s_pg, NEG_INF)
        yield
    s = s_ref[...]
    m_old = m_ref[...]
    m_new = jnp.maximum(m_old, jnp.max(s, axis=-1, keepdims=True))
    alpha = jnp.exp2(m_old - m_new)
    p = jnp.exp2(s - m_new)
    l_ref[...] = alpha * l_ref[...] + jnp.sum(p, axis=-1, keepdims=True)
    m_ref[...] = m_new
    s_ref[...] = p
    yield
    pv = None
    for pg in range(n_pages):
        p_pg = s_ref[:, pg * page_rows:(pg + 1) * page_rows].astype(BF16)
        d = _dot(p_pg, v_refs[pg][...].astype(BF16))
        pv = d if pv is None else pv + d
        yield
    acc_ref[...] = alpha * acc_ref[...] + pv

    @pl.when(last)
    def _():
        lam = _diff_lambda(lq1_ref[...], lk1_ref[...], lq2_ref[...], lk2_ref[...])
        kn = kn_ref[...].astype(BF16).astype(F32)
        vn = vn_ref[...].astype(BF16).astype(F32)
        for c in range(2):
            new_ref[0, c * n_heads:(c + 1) * n_heads, :] = kn
            new_ref[1, c * n_heads:(c + 1) * n_heads, :] = vn
        s_new = jnp.sum(qm_ref[...] * new_ref[0], axis=-1, keepdims=True)
        m_old = m_ref[...]
        m_new = jnp.maximum(m_old, s_new)
        alpha = jnp.exp2(m_old - m_new)
        p_new = jnp.exp2(s_new - m_new)
        l = alpha * l_ref[...] + p_new
        acc = alpha * acc_ref[...] + p_new.astype(BF16).astype(F32) * new_ref[1]
        o = acc / l
        o = _sub_norm(o[0:n_heads, :] - lam * o[n_heads:rows, :], sg_ref[...])
        for h in range(n_heads):
            o_ref[:, h * hw:(h + 1) * hw] = o[h:h + 1, :].astype(o_ref.dtype)


def _attn_sample(qb, k_new, v_new, lam_p, subln_g, cache_k, cache_v, page_table, n_heads):
    n, aw = qb.shape
    hw = aw // n_heads
    page = cache_k.shape[1]
    n_log = page_table.shape[1]
    pp = DEC_PAGES
    while n_log % pp:
        pp //= 2
    row3 = lambda a: a.reshape(n, 1, aw)
    row_spec = pl.BlockSpec((None, 1, aw), lambda b, p, pt: (b, 0, 0))
    new_spec = pl.BlockSpec((None, n_heads, hw), lambda b, p, pt: (b, 0, 0))
    vec = pl.BlockSpec((1, HEAD_DIM), lambda b, p, pt: (0, 0))

    def page_spec(j):
        return pl.BlockSpec((None, page * n_heads, hw), lambda b, p, pt: (pt[b, p * pp + j], 0, 0))

    grid_spec = pltpu.PrefetchScalarGridSpec(
        num_scalar_prefetch=1,
        grid=(n, n_log // pp),
        in_specs=[row_spec, new_spec, new_spec, vec, vec, vec, vec,
                  pl.BlockSpec((1, hw), lambda b, p, pt: (0, 0))]
                 + [page_spec(j) for j in range(pp)] + [page_spec(j) for j in range(pp)],
        out_specs=row_spec,
        scratch_shapes=[
            pltpu.VMEM((2 * n_heads, hw), F32),
            pltpu.VMEM((2, 2 * n_heads, hw), F32),
            pltpu.VMEM((2 * n_heads, pp * page * n_heads), F32),
            pltpu.VMEM((2 * n_heads, 1), F32),
            pltpu.VMEM((2 * n_heads, 1), F32),
            pltpu.VMEM((2 * n_heads, hw), F32),
        ],
    )
    ck = cache_k.reshape(cache_k.shape[0], page * n_heads, hw)
    cv = cache_v.reshape(cache_v.shape[0], page * n_heads, hw)
    out = pl.pallas_call(
        functools.partial(_attn_sample_kernel, n_pages=pp, n_heads=n_heads),
        grid_spec=grid_spec,
        out_shape=jax.ShapeDtypeStruct((n, 1, aw), BF16),
        compiler_params=_params("parallel", "arbitrary"),
        name="attn_sample",
    )(page_table, row3(qb), k_new.reshape(n, n_heads, hw), v_new.reshape(n, n_heads, hw),
      *lam_p, subln_g, *([ck] * pp), *([cv] * pp))
    return out.reshape(n, aw)


def _mixer_decode_kernel(pt_ref, *refs, tm, pad, ka, kb, n_pages, n_heads, groups):
    n_dec_in = N_DECODE_IN + 2 * n_pages
    mixer_in = refs[:N_MIXER_IN]
    dec_in = refs[N_MIXER_IN:N_MIXER_IN + n_dec_in]
    outs = refs[N_MIXER_IN + n_dec_in:N_MIXER_IN + n_dec_in + 5]
    scratch = refs[N_MIXER_IN + n_dec_in + 5:]
    step = pl.program_id(0) * pl.num_programs(1) + pl.program_id(1)
    group = step % groups
    _trace_interleaved(
        _decode_pages(group == 0, group == groups - 1, *dec_in, outs[4], *scratch[6:],
                      n_pages=n_pages, n_heads=n_heads),
        _mixer_tile(pl.program_id(1) == 0, *mixer_in, *outs[:4], *scratch[:6],
                    tm=tm, pad=pad, ka=ka, kb=kb))


def _fused_page_group(bsz, t, n_seq, n_log):
    tm = min(FUSED_MIX_ROWS, t)
    if t % tm:
        return None
    steps = bsz * (t // tm)
    if steps % n_seq or n_log % (steps // n_seq):
        return None
    pp = n_log // (steps // n_seq)
    return pp if pp <= FUSED_MAX_PAGES else None


def _mixer_decode(x, h0, ca0, cb0, mp, g, b, qb, k_new, v_new, lam_p, subln_g,
                  cache_k, cache_v, page_table, n_heads):
    bsz, t, d = x.shape
    d_a = h0.shape[-1]
    ka = ca0.shape[1] + 1
    kb = cb0.shape[1] + 1
    assert ka - 1 <= HIST_A and kb - 1 <= HIST_B
    n, aw = qb.shape
    hw = aw // n_heads
    page = cache_k.shape[1]
    n_log = page_table.shape[1]
    pp = _fused_page_group(bsz, t, n, n_log)
    tm = min(FUSED_MIX_ROWS, t)
    nt = t // tm
    groups = n_log // pp
    pad = max(SUBLANES, tm // 2)

    seq = lambda bi, ti: (bi * nt + ti) // groups
    bt = lambda bi, ti, pt: (bi, ti, 0)
    b0 = lambda bi, ti, pt: (bi, 0, 0)
    s0 = lambda bi, ti, pt: (seq(bi, ti), 0, 0)

    def page_spec(j):
        return pl.BlockSpec(
            (None, page * n_heads, hw),
            lambda bi, ti, pt: (pt[seq(bi, ti), ((bi * nt + ti) % groups) * pp + j], 0, 0))

    vec = _const_spec((1, HEAD_DIM))
    in_specs = [
        pl.BlockSpec((None, tm, d), bt),
        pl.BlockSpec((None, 1, d_a), b0),
        pl.BlockSpec((None, ka - 1, d_a), b0),
        pl.BlockSpec((None, kb - 1, d_a), b0),
        _resident_spec(mp["w_in"].shape),
        _const_spec((ka, d_a)), _const_spec((1, d_a)),
        _resident_spec((d_a, d_a)), _const_spec((1, d_a)),
        _resident_spec((d_a, d_a)), _const_spec((1, d_a)),
        _const_spec((1, d_a)),
        _const_spec((kb, d_a)), _const_spec((1, d_a)),
        _const_spec((1, d_a)), _const_spec((1, d_a)),
        _resident_spec(mp["w_out"].shape),
        _const_spec((1, d)), _const_spec((1, d)),
        pl.BlockSpec((None, 1, aw), s0),
        pl.BlockSpec((None, n_heads, hw), s0),
        pl.BlockSpec((None, n_heads, hw), s0),
        vec, vec, vec, vec, _const_spec((1, hw)),
    ] + [page_spec(j) for j in range(pp)] + [page_spec(j) for j in range(pp)]
    assert len(in_specs) == N_MIXER_IN + N_DECODE_IN + 2 * pp
    grid_spec = pltpu.PrefetchScalarGridSpec(
        num_scalar_prefetch=1,
        grid=(bsz, nt),
        in_specs=in_specs,
        out_specs=[
            pl.BlockSpec((None, tm, d), bt),
            pl.BlockSpec((None, 1, d_a), b0),
            pl.BlockSpec((None, ka - 1, d_a), b0),
            pl.BlockSpec((None, kb - 1, d_a), b0),
            pl.BlockSpec((None, 1, aw), s0),
        ],
        scratch_shapes=[
            pltpu.VMEM((HIST_A + tm, d_a), F32),
            pltpu.VMEM((HIST_B + tm, d_a), F32),
            pltpu.VMEM((SUBLANES - 1, HIST_B - SUBLANES + tm, d_a), F32),
            pltpu.VMEM((pad + tm, d_a), F32),
            pltpu.VMEM((pad + tm, d_a), F32),
            pltpu.VMEM((1, d_a), F32),
            pltpu.VMEM((2 * n_heads, hw), F32),
            pltpu.VMEM((2, 2 * n_heads, hw), F32),
            pltpu.VMEM((2 * n_heads, pp * page * n_heads), F32),
            pltpu.VMEM((2 * n_heads, 1), F32),
            pltpu.VMEM((2 * n_heads, 1), F32),
            pltpu.VMEM((2 * n_heads, hw), F32),
        ],
    )
    ck = cache_k.reshape(cache_k.shape[0], page * n_heads, hw)
    cv = cache_v.reshape(cache_v.shape[0], page * n_heads, hw)
    y, hl, cas, cbs, att = pl.pallas_call(
        functools.partial(_mixer_decode_kernel, tm=tm, pad=pad, ka=ka, kb=kb,
                          n_pages=pp, n_heads=n_heads, groups=groups),
        grid_spec=grid_spec,
        out_shape=[
            jax.ShapeDtypeStruct((bsz, t, d), F32),
            jax.ShapeDtypeStruct((bsz, 1, d_a), F32),
            jax.ShapeDtypeStruct((bsz, ka - 1, d_a), F32),
            jax.ShapeDtypeStruct((bsz, kb - 1, d_a), F32),
            jax.ShapeDtypeStruct((n, 1, aw), BF16),
        ],
        compiler_params=_params("arbitrary", "arbitrary"),
        name="mixer_decode",
    )(page_table, x, h0.reshape(bsz, 1, d_a), ca0, cb0, mp["w_in"], mp["conv_a_w"], mp["conv_a_b"],
      mp["wa"], mp["ba"], mp["wx"], mp["bx"], mp["lam"], mp["conv_b_w"], mp["conv_b_b"],
      mp["cln_g"], mp["cln_b"], mp["w_out"], g, b,
      qb.reshape(n, 1, aw), k_new.reshape(n, n_heads, hw), v_new.reshape(n, n_heads, hw),
      *lam_p, subln_g, *([ck] * pp), *([cv] * pp))
    return y, hl.reshape(bsz, d_a), cas, cbs, att.reshape(n, aw)


def _proj_ln_kernel(a_ref, x_ref, w_ref, g_ref, b_ref, o_ref):
    y = _dot(a_ref[...], w_ref[...])
    o_ref[...] = _layer_norm(ALPHA * x_ref[...] + y, g_ref[...], b_ref[...])


def _proj_ln(a, x, w, g, b):
    m, d = x.shape
    tm = min(PROJ_ROWS, m)
    return pl.pallas_call(
        _proj_ln_kernel,
        grid=(pl.cdiv(m, tm),),
        in_specs=[pl.BlockSpec((tm, a.shape[1]), lambda i: (i, 0)),
                  pl.BlockSpec((tm, d), lambda i: (i, 0)),
                  _const_spec(w.shape), _const_spec((1, d)), _const_spec((1, d))],
        out_specs=pl.BlockSpec((tm, d), lambda i: (i, 0)),
        out_shape=jax.ShapeDtypeStruct((m, d), F32),
        compiler_params=_params("parallel"),
        name="proj_ln",
    )(a, x, w, g, b)


def _block_diag(w):
    n, bi, bj = w.shape
    eye = jnp.eye(n, dtype=w.dtype)
    return jnp.einsum("nij,nm->nimj", w, eye).reshape(n * bi, n * bj)


def kernel(x_prompt, x_sample, state_lru_h, state_conv_a, state_conv_b, cache_k, cache_v, page_table,
           ln_g, ln_b, ffn_w_gate, ffn_w_up, ffn_w_down,
           rec_w_in, conv_a_w, conv_a_b, lru_w_a, lru_b_a, lru_w_x, lru_b_x, lru_lambda,
           conv_b_w, conv_b_b, conv_ln_g, conv_ln_b, rec_w_out,
           att_w_qkv, lambda_q1, lambda_k1, lambda_q2, lambda_k2, subln_g, att_w_out):
    bp, tp, d = x_prompt.shape
    bs, ts, _ = x_sample.shape
    assert ts == 1, "the sample group decodes one token per sequence"
    n_heads = cache_k.shape[2]
    d_a = state_lru_h.shape[-1]
    d_ff = ffn_w_gate.shape[-1]
    assert d_ff % FFN_CHUNK == 0
    row = lambda v: v.reshape(1, -1)

    ffn_w = (ffn_w_gate.astype(BF16), ffn_w_up.astype(BF16), ffn_w_down.astype(BF16))
    lng = [[row(ln_g[l, i]) for i in range(3)] for l in range(DEPTH)]
    lnb = [[row(ln_b[l, i]) for i in range(3)] for l in range(DEPTH)]
    mp = dict(
        w_in=rec_w_in.astype(BF16), conv_a_w=conv_a_w, conv_a_b=row(conv_a_b),
        wa=_block_diag(lru_w_a).astype(BF16), ba=row(lru_b_a),
        wx=_block_diag(lru_w_x).astype(BF16), bx=row(lru_b_x), lam=row(lru_lambda),
        conv_b_w=conv_b_w, conv_b_b=row(conv_b_b), cln_g=row(conv_ln_g), cln_b=row(conv_ln_b),
        w_out=rec_w_out.astype(BF16))
    w_qkv = att_w_qkv.astype(BF16)
    w_o = att_w_out.astype(BF16)
    lam_p = (row(lambda_q1), row(lambda_k1), row(lambda_q2), row(lambda_k2))
    sg = row(subln_g)

    def ffn_sub(x, l, i, proj=None):
        return _ffn_sublayer(x, *ffn_w, l, i, lng[l][2 * i], lnb[l][2 * i], proj)

    past = page_table.shape[1] * cache_k.shape[1]
    xs = x_sample.reshape(bs, d)
    xs = ffn_sub(xs, 0, 0)
    xs, h_s, ca_s, cb_s = _mixer_sample(xs, state_lru_h, state_conv_a, state_conv_b, mp,
                                        lng[0][1], lnb[0][1])
    xs = ffn_sub(xs, 0, 1)
    xs = ffn_sub(xs, 1, 0)
    cos_s, sin_s = _rope_tables(past + jnp.arange(ts, dtype=jnp.int32))
    k_s, v_s, qsb = _qkv_rope(xs, w_qkv, cos_s, sin_s, n_heads, ts)

    x = x_prompt.reshape(bp * tp, d)
    x = ffn_sub(x, 0, 0).reshape(bp, tp, d)
    zero_state = (jnp.zeros((bp, d_a), F32), jnp.zeros((bp, state_conv_a.shape[1], d_a), F32),
                  jnp.zeros((bp, state_conv_b.shape[1], d_a), F32))
    if _fused_page_group(bp, tp, bs, page_table.shape[1]) is not None:
        x, h_p, ca_p, cb_p, att_s = _mixer_decode(
            x, *zero_state, mp, lng[0][1], lnb[0][1], qsb, k_s, v_s, lam_p, sg,
            cache_k, cache_v, page_table, n_heads)
    else:
        x, h_p, ca_p, cb_p = _mixer_prompt(x, *zero_state, mp, lng[0][1], lnb[0][1])
        att_s = _attn_sample(qsb, k_s, v_s, lam_p, sg, cache_k, cache_v, page_table, n_heads)
    x = ffn_sub(x.reshape(bp * tp, d), 0, 1)
    x = ffn_sub(x, 1, 0)
    cos_p, sin_p = _rope_tables(jnp.arange(tp, dtype=jnp.int32))
    att_tile = min(ATT_TILE, tp)
    k_p, v_p, qb, kb, vt = _qkv_rope(x, w_qkv, cos_p, sin_p, n_heads, tp, att_tile)
    att = _attn_prompt(qb, kb, vt, lam_p, sg, bp, tp, n_heads, att_tile)
    y_prompt = ffn_sub(x, 1, 1, (att, w_o, lng[1][1], lnb[1][1])).reshape(bp, tp, d)

    y_sample = ffn_sub(xs, 1, 1, (att_s, w_o, lng[1][1], lnb[1][1])).reshape(bs, ts, d)

    hw = 2 * HEAD_DIM
    return (y_prompt, y_sample, h_p, ca_p, cb_p,
            k_p.reshape(bp, tp, n_heads, hw), v_p.reshape(bp, tp, n_heads, hw),
            h_s, ca_s, cb_s,
            k_s.reshape(bs, ts, n_heads, hw), v_s.reshape(bs, ts, n_heads, hw))
```
